```python
import jax, jax.numpy as jnp
from jax import lax
import numpy as np

D_MODEL = 1024
BATCH = 8
SEQ = 4096
DEPTH = 2

RET_HEADS = 4
RET_HEAD_DIM = D_MODEL // RET_HEADS
RET_WIDTH = RET_HEADS * RET_HEAD_DIM
RET_CHUNK = 128
ROPE_BASE = 10000.0

CONV_CH = D_MODEL
CONV_K = 3

SWA_HEAD_DIM = 64
SWA_Q_HEADS = D_MODEL // SWA_HEAD_DIM
SWA_KV_HEADS = SWA_Q_HEADS // 8
SWA_WIDTH = SWA_Q_HEADS * SWA_HEAD_DIM
SWA_WINDOW = 128
SWA_BLOCK = 128

N_BRANCH = 3
D_FF = 2816
MACARON_WEIGHT = 0.5
N_MOD = 9
EPS = 1e-6

RET_COLS = 4 * RET_WIDTH
CONV_COLS = 3 * CONV_CH
SWA_COLS = (SWA_Q_HEADS + 2 * SWA_KV_HEADS) * SWA_HEAD_DIM
GATE_COLS = N_BRANCH * D_MODEL
OFF_CONV = RET_COLS
OFF_SWA = OFF_CONV + CONV_COLS
OFF_GATE = OFF_SWA + SWA_COLS
IN_COLS = OFF_GATE + GATE_COLS

kernel_name = 'hybrid_retention_shortconv_swa_macaron'


def rmsnorm(t, g):
    tf = t.astype(jnp.float32)
    n = tf * lax.rsqrt(jnp.mean(tf * tf, axis=-1, keepdims=True) + EPS)
    return n.astype(t.dtype) * g


def modulate(t, shift, scale):
    return t * (1 + scale[:, None, :]) + shift[:, None, :]


def swiglu(t, w1, w3, w2):
    return (jax.nn.silu(t @ w1) * (t @ w3)) @ w2


def rotary(t):
    S, d = t.shape[1], t.shape[-1]
    half = d // 2
    inv = jnp.power(ROPE_BASE, -jnp.linspace(0.0, 1.0, half, dtype=jnp.float32))
    ang = jnp.arange(S, dtype=jnp.float32)[:, None] * inv[None, :]
    cos = jnp.cos(ang)[None, :, None, :]
    sin = jnp.sin(ang)[None, :, None, :]
    t1, t2 = t[..., :half], t[..., half:]
    return jnp.concatenate([t1 * cos - t2 * sin, t2 * cos + t1 * sin], axis=-1)


def retention(q, k, v):
    B, S, H, dk = q.shape
    dv = v.shape[-1]
    C = RET_CHUNK
    N = S // C
    log_gamma = jnp.log1p(-jnp.exp2(-5.0 - jnp.arange(H, dtype=jnp.float32)))
    idx = jnp.arange(C, dtype=jnp.float32)
    rel = idx[:, None] - idx[None, :]
    intra = jnp.where(rel >= 0, jnp.exp(log_gamma[:, None, None] * jnp.maximum(rel, 0.0)), 0.0)
    xi = jnp.exp(log_gamma[:, None] * (idx + 1.0))[None, :, :, None]
    zeta = jnp.exp(log_gamma[:, None] * (C - 1.0 - idx))[None, :, :, None]
    chunk_decay = jnp.exp(log_gamma * C)[None, :, None, None]
    k = k * (dk ** -0.5)

    def to_chunks(t):
        return t.reshape(B, N, C, H, t.shape[-1]).transpose(1, 0, 3, 2, 4)

    def step(state, inp):
        qc, kc, vc = inp
        scores = jnp.einsum('bhid,bhjd->bhij', qc, kc) * intra[None]
        inner = jnp.einsum('bhij,bhje->bhie', scores, vc)
        cross = jnp.einsum('bhid,bhde->bhie', qc, state) * xi
        new_state = state * chunk_decay + jnp.einsum('bhjd,bhje->bhde', kc * zeta, vc)
        return new_state, inner + cross

    state0 = jnp.zeros((B, H, dk, dv), jnp.float32)
    _, out = lax.scan(step, state0, (to_chunks(q), to_chunks(k), to_chunks(v)))
    return out.transpose(1, 0, 3, 2, 4).reshape(B, S, H, dv)


def short_conv(z, w):
    S = z.shape[1]
    zp = jnp.pad(z, ((0, 0), (CONV_K - 1, 0), (0, 0)))
    y = zp[:, 0:S, :] * w[0]
    for j in range(1, CONV_K):
        y = y + zp[:, j:j + S, :] * w[j]
    return y


def swa_with_sinks(q, k, v, sinks):
    B, S, Hq, d = q.shape
    Hkv = k.shape[2]
    G = Hq // Hkv
    W = SWA_BLOCK
    N = S // W
    qb = q.reshape(B, N, W, Hkv, G, d)

    def band(t):
        tp = jnp.pad(t, ((0, 0), (W, 0), (0, 0), (0, 0)))
        prev = tp[:, :S].reshape(B, N, W, Hkv, d)
        cur = t.reshape(B, N, W, Hkv, d)
        return jnp.concatenate([prev, cur], axis=2)

    kb, vb = band(k), band(v)
    scores = jnp.einsum('bnqkgd,bnjkd->bnkgqj', qb, kb).astype(jnp.float32) * (d ** -0.5)
    i = jnp.arange(W)[:, None]
    j = jnp.arange(2 * W)[None, :]
    diff = i + W - j
    in_band = (diff >= 0) & (diff < SWA_WINDOW)
    valid = in_band[None] & ((jnp.arange(N)[:, None, None] > 0) | (j >= W)[None])
    scores = jnp.where(valid[None, :, None, None], scores, -jnp.inf)
    s = sinks.astype(jnp.float32).reshape(Hkv, G)[None, None, :, :, None, None]
    m = jnp.maximum(jnp.max(scores, axis=-1, keepdims=True), s)
    p = jnp.exp(scores - m)
    probs = p / (jnp.sum(p, axis=-1, keepdims=True) + jnp.exp(s - m))
    out = jnp.einsum('bnkgqj,bnjkd->bnqkgd', probs.astype(v.dtype), vb)
    return out.reshape(B, S, Hq * d)


def hybrid_mixer(u, w_in, conv_w, sinks, p_ret, p_conv, p_swa, w_out):
    B, S, _ = u.shape
    proj = u @ w_in
    rq = proj[..., 0:RET_WIDTH].reshape(B, S, RET_HEADS, RET_HEAD_DIM).astype(jnp.float32)
    rk = proj[..., RET_WIDTH:2 * RET_WIDTH].reshape(B, S, RET_HEADS, RET_HEAD_DIM).astype(jnp.float32)
    rv = proj[..., 2 * RET_WIDTH:3 * RET_WIDTH].reshape(B, S, RET_HEADS, RET_HEAD_DIM).astype(jnp.float32)
    rg = proj[..., 3 * RET_WIDTH:RET_COLS]
    r = retention(rotary(rq), rotary(rk), rv)
    r = r * lax.rsqrt(jnp.mean(r * r, axis=-1, keepdims=True) + EPS)
    y_a = jax.nn.silu(rg) * r.reshape(B, S, RET_WIDTH).astype(u.dtype)
    cb = proj[..., OFF_CONV:OFF_CONV + CONV_CH]
    cc = proj[..., OFF_CONV + CONV_CH:OFF_CONV + 2 * CONV_CH]
    cx = proj[..., OFF_CONV + 2 * CONV_CH:OFF_SWA]
    y_b = cb * short_conv(cc * cx, conv_w)
    kv_w = SWA_KV_HEADS * SWA_HEAD_DIM
    sq = proj[..., OFF_SWA:OFF_SWA + SWA_WIDTH].reshape(B, S, SWA_Q_HEADS, SWA_HEAD_DIM)
    sk = proj[..., OFF_SWA + SWA_WIDTH:OFF_SWA + SWA_WIDTH + kv_w].reshape(B, S, SWA_KV_HEADS, SWA_HEAD_DIM)
    sv = proj[..., OFF_SWA + SWA_WIDTH + kv_w:OFF_GATE].reshape(B, S, SWA_KV_HEADS, SWA_HEAD_DIM)
    y_c = swa_with_sinks(sq, sk, sv, sinks)
    gates = jax.nn.sigmoid(proj[..., OFF_GATE:IN_COLS]).reshape(B, S, N_BRANCH, D_MODEL)
    merged = (gates[:, :, 0] * (y_a @ p_ret)
              + gates[:, :, 1] * (y_b @ p_conv)
              + gates[:, :, 2] * (y_c @ p_swa))
    return merged @ w_out


def setup_inputs(seed: int = 0) -> dict:
    key = jax.random.key(seed)
    ks = jax.random.split(key, 24)
    f32 = jnp.float32

    def nrm(k, shape, scale):
        return jax.random.normal(k, shape, f32) * scale

    def gain(k, shape):
        return 1.0 + 0.05 * jax.random.normal(k, shape, f32)

    Dm = D_MODEL
    return {
        'x': nrm(ks[0], (BATCH, SEQ, Dm), 1.0),
        'c': nrm(ks[1], (BATCH, Dm), 1.0),
        'ada_w': nrm(ks[2], (DEPTH, Dm, N_MOD * Dm), 0.5 * Dm ** -0.5),
        'ada_b': nrm(ks[3], (DEPTH, N_MOD * Dm), 0.02),
        'ffn1_norm': gain(ks[4], (DEPTH, Dm)),
        'ffn1_w1': nrm(ks[5], (DEPTH, Dm, D_FF), Dm ** -0.5),
        'ffn1_w3': nrm(ks[6], (DEPTH, Dm, D_FF), Dm ** -0.5),
        'ffn1_w2': nrm(ks[7], (DEPTH, D_FF, Dm), D_FF ** -0.5),
        'mix_norm': gain(ks[8], (DEPTH, Dm)),
        'w_in': nrm(ks[9], (DEPTH, Dm, IN_COLS), Dm ** -0.5),
        'conv_w': nrm(ks[10], (DEPTH, CONV_K, CONV_CH), CONV_K ** -0.5),
        'swa_sinks': nrm(ks[11], (DEPTH, SWA_Q_HEADS), 1.0),
        'p_ret': nrm(ks[12], (DEPTH, RET_WIDTH, Dm), RET_WIDTH ** -0.5),
        'p_conv': nrm(ks[13], (DEPTH, CONV_CH, Dm), CONV_CH ** -0.5),
        'p_swa': nrm(ks[14], (DEPTH, SWA_WIDTH, Dm), SWA_WIDTH ** -0.5),
        'w_out': nrm(ks[15], (DEPTH, Dm, Dm), Dm ** -0.5),
        'ffn2_norm': gain(ks[16], (DEPTH, Dm)),
        'ffn2_w1': nrm(ks[17], (DEPTH, Dm, D_FF), Dm ** -0.5),
        'ffn2_w3': nrm(ks[18], (DEPTH, Dm, D_FF), Dm ** -0.5),
        'ffn2_w2': nrm(ks[19], (DEPTH, D_FF, Dm), D_FF ** -0.5),
        'final_norm': gain(ks[20], (Dm,)),
    }


def reference(x, c, ada_w, ada_b, ffn1_norm, ffn1_w1, ffn1_w3, ffn1_w2, mix_norm, w_in,
              conv_w, swa_sinks, p_ret, p_conv, p_swa, w_out, ffn2_norm, ffn2_w1, ffn2_w3,
              ffn2_w2, final_norm):
    h = x
    c_act = jax.nn.silu(c)
    for l in range(DEPTH):
        mod = c_act @ ada_w[l] + ada_b[l]
        sh1, sc1, g1, sh2, sc2, g2, sh3, sc3, g3 = jnp.split(mod, N_MOD, axis=-1)
        u = modulate(rmsnorm(h, ffn1_norm[l]), sh1, sc1)
        h = h + MACARON_WEIGHT * g1[:, None, :] * swiglu(u, ffn1_w1[l], ffn1_w3[l], ffn1_w2[l])
        u = modulate(rmsnorm(h, mix_norm[l]), sh2, sc2)
        h = h + g2[:, None, :] * hybrid_mixer(u, w_in[l], conv_w[l], swa_sinks[l], p_ret[l],
                                              p_conv[l], p_swa[l], w_out[l])
        u = modulate(rmsnorm(h, ffn2_norm[l]), sh3, sc3)
        h = h + MACARON_WEIGHT * g3[:, None, :] * swiglu(u, ffn2_w1[l], ffn2_w3[l], ffn2_w2[l])
    return rmsnorm(h, final_norm)
```

```python
import functools

import jax
import jax.numpy as jnp
from jax import lax
from jax.experimental import pallas as pl
from jax.experimental.pallas import tpu as pltpu

F32 = jnp.float32
BF16 = jnp.bfloat16

D_MODEL = 1024
N_MOD = 9
EPS = 1e-6
MACARON_WEIGHT = 0.5

RET_HEADS = 4
RET_DIM = 256
RET_WIDTH = RET_HEADS * RET_DIM
ROPE_BASE = 10000.0
ROPE_HALF = RET_DIM // 2

CONV_K = 3

SWA_DIM = 64
SWA_Q_HEADS = 16
SWA_KV_HEADS = 2
SWA_GROUP = SWA_Q_HEADS // SWA_KV_HEADS
SWA_BLOCK = 128
SWA_KV_W = SWA_KV_HEADS * SWA_DIM

OFF_RQ = 0
OFF_RK = RET_WIDTH
OFF_RV = 2 * RET_WIDTH
OFF_RG = 3 * RET_WIDTH
OFF_CB = 4 * RET_WIDTH
OFF_CC = OFF_CB + D_MODEL
OFF_CX = OFF_CC + D_MODEL
OFF_SQ = OFF_CX + D_MODEL
OFF_SKV = OFF_SQ + SWA_Q_HEADS * SWA_DIM
OFF_GATE = OFF_SKV + 2 * SWA_KV_W
IN_COLS = OFF_GATE + 3 * D_MODEL

VMEM_LIMIT_BYTES = 56 * 1024 * 1024

FFN_TOKENS = 512
FFN_CHUNK = 256
MIXA_TOKENS = 256
MIXB_TOKENS = 256
RET_CHUNK = 256


def _resident(shape):
    nd = len(shape)
    return pl.BlockSpec(shape, lambda *_: (0,) * nd, pipeline_mode=pl.Buffered(1))


def _row_vec(d):
    return pl.BlockSpec((None, 1, d), lambda b, i: (b, 0, 0))


def _norm_modulate(x, g, shift, scale):
    ms = jnp.mean(x * x, axis=-1, keepdims=True)
    n = x * lax.rsqrt(ms + EPS)
    return (n * g) * (1.0 + scale) + shift


def _silu(x):
    return x * jax.nn.sigmoid(x)


def _dot(a, b):
    return jnp.dot(a, b, preferred_element_type=F32)


def _dot_nt(a, b):
    return lax.dot_general(a, b, (((1,), (1,)), ((), ())), preferred_element_type=F32)


def _dot_tn(a, b):
    return lax.dot_general(a, b, (((0,), (0,)), ((), ())), preferred_element_type=F32)


def _mod_kernel(c_ref, w_ref, b_ref, o_ref):
    ca = _silu(c_ref[...]).astype(BF16)
    o_ref[...] = _dot(ca, w_ref[...].astype(BF16)) + b_ref[...]


def _modulation(c, ada_w, ada_b):
    depth = ada_w.shape[0]
    bsz, d = c.shape
    b4 = ada_b.reshape(depth, N_MOD, 1, d)
    return pl.pallas_call(
        _mod_kernel,
        grid=(depth, N_MOD),
        in_specs=[
            pl.BlockSpec((bsz, d), lambda l, j: (0, 0)),
            pl.BlockSpec((None, d, d), lambda l, j: (l, 0, j)),
            pl.BlockSpec((None, None, 1, d), lambda l, j: (l, j, 0, 0)),
        ],
        out_specs=pl.BlockSpec((None, None, bsz, d), lambda l, j: (l, j, 0, 0)),
        out_shape=jax.ShapeDtypeStruct((depth, N_MOD, bsz, d), F32),
        compiler_params=pltpu.CompilerParams(
            dimension_semantics=("arbitrary", "arbitrary")),
        name="adaln_modulation",
    )(c, ada_w, b4)


def _ffn_kernel(h_ref, sh_ref, sc_ref, gt_ref, g_ref, w13_ref, w2_ref, *rest,
                n_chunks, chunk, final):
    if final:
        fg_ref, o_ref, u_ref, acc_ref = rest
    else:
        o_ref, u_ref, acc_ref = rest
    x = h_ref[...]
    u_ref[...] = _norm_modulate(x, g_ref[...], sh_ref[...], sc_ref[...]).astype(BF16)
    acc_ref[...] = jnp.zeros_like(acc_ref)

    def body(j, carry):
        ab = _dot(u_ref[...], w13_ref[j])
        a = ab[:, :chunk]
        b = ab[:, chunk:]
        act = (_silu(a) * b).astype(BF16)
        acc_ref[...] += _dot(act, w2_ref[j])
        return carry

    lax.fori_loop(0, n_chunks, body, 0)
    y = x + (MACARON_WEIGHT * gt_ref[...]) * acc_ref[...]
    if final:
        ms = jnp.mean(y * y, axis=-1, keepdims=True)
        y = (y * lax.rsqrt(ms + EPS)) * fg_ref[...]
    o_ref[...] = y


def _ffn(h, shift, scale, gate, norm_g, w13, w2, final_g=None):
    bsz, seq, d = h.shape
    n_chunks, _, two_chunk = w13.shape
    chunk = two_chunk // 2
    tm = min(FFN_TOKENS, seq)
    final = final_g is not None
    in_specs = [
        pl.BlockSpec((None, tm, d), lambda b, i: (b, i, 0)),
        _row_vec(d), _row_vec(d), _row_vec(d),
        _resident((1, d)),
        _resident(w13.shape),
        _resident(w2.shape),
    ]
    args = [h, shift, scale, gate, norm_g, w13, w2]
    if final:
        in_specs.append(_resident((1, d)))
        args.append(final_g)
    return pl.pallas_call(
        functools.partial(_ffn_kernel, n_chunks=n_chunks, chunk=chunk, final=final),
        grid=(bsz, seq // tm),
        in_specs=in_specs,
        out_specs=pl.BlockSpec((None, tm, d), lambda b, i: (b, i, 0)),
        out_shape=jax.ShapeDtypeStruct((bsz, seq, d), F32),
        scratch_shapes=[pltpu.VMEM((tm, d), BF16), pltpu.VMEM((tm, d), F32)],
        compiler_params=pltpu.CompilerParams(
            dimension_semantics=("arbitrary", "arbitrary"),
            vmem_limit_bytes=VMEM_LIMIT_BYTES),
        name="swiglu_ffn_final" if final else "swiglu_ffn",
    )(*args)


def _mixa_kernel(h_ref, sh_ref, sc_ref, g_ref, w_ref, cw_ref, cos_ref, sin_ref,
                 qk_ref, v_ref, sg_ref, yb_ref, sq_ref, skv_ref, gt_ref,
                 u_ref, zc_ref, *, tokens):
    d = D_MODEL
    u_ref[...] = _norm_modulate(h_ref[...], g_ref[...], sh_ref[...], sc_ref[...]).astype(BF16)

    def proj(off, width):
        return _dot(u_ref[...], w_ref[:, off:off + width])

    cos = cos_ref[...]
    sin = sin_ref[...]
    for hd in range(2 * RET_HEADS):
        p = proj(hd * RET_DIM, RET_DIM)
        p1 = p[:, :ROPE_HALF]
        p2 = p[:, ROPE_HALF:]
        o1 = p1 * cos - p2 * sin
        o2 = p2 * cos + p1 * sin
        if hd >= RET_HEADS:
            o1 = o1 * (RET_DIM ** -0.5)
            o2 = o2 * (RET_DIM ** -0.5)
        qk_ref[:, hd * RET_DIM:hd * RET_DIM + ROPE_HALF] = o1.astype(BF16)
        qk_ref[:, hd * RET_DIM + ROPE_HALF:(hd + 1) * RET_DIM] = o2.astype(BF16)

    v_ref[...] = proj(OFF_RV, RET_WIDTH).astype(BF16)
    sg_ref[...] = _silu(proj(OFF_RG, RET_WIDTH)).astype(BF16)

    @pl.when(pl.program_id(1) == 0)
    def _():
        zc_ref[...] = jnp.zeros_like(zc_ref)

    z = proj(OFF_CC, d) * proj(OFF_CX, d)
    zcat = jnp.concatenate([zc_ref[...], z], axis=0)
    z1 = pltpu.roll(zcat, 1, axis=0)[8:]
    z2 = pltpu.roll(zcat, 2, axis=0)[8:]
    conv = z2 * cw_ref[0:1, :] + z1 * cw_ref[1:2, :] + z * cw_ref[2:3, :]
    zc_ref[...] = z[tokens - 8:]
    yb_ref[...] = (proj(OFF_CB, d) * conv).astype(BF16)

    sq_ref[...] = (proj(OFF_SQ, d) * (SWA_DIM ** -0.5)).astype(BF16)
    skv_ref[...] = proj(OFF_SKV, 2 * SWA_KV_W).astype(BF16)

    for j in range(3):
        gt_ref[:, j * d:(j + 1) * d] = jax.nn.sigmoid(proj(OFF_GATE + j * d, d)).astype(BF16)


def _mixer_project(h, shift, scale, norm_g, w_in, conv_w, cos, sin):
    bsz, seq, d = h.shape
    t = min(MIXA_TOKENS, seq)

    def tok(width):
        return pl.BlockSpec((None, t, width), lambda b, i: (b, i, 0))

    widths = [2 * RET_WIDTH, RET_WIDTH, RET_WIDTH, d, d, 2 * SWA_KV_W, 3 * d]
    return pl.pallas_call(
        functools.partial(_mixa_kernel, tokens=t),
        grid=(bsz, seq // t),
        in_specs=[
            tok(d), _row_vec(d), _row_vec(d),
            _resident((1, d)),
            _resident(w_in.shape),
            _resident(conv_w.shape),
            pl.BlockSpec((t, ROPE_HALF), lambda b, i: (i, 0)),
            pl.BlockSpec((t, ROPE_HALF), lambda b, i: (i, 0)),
        ],
        out_specs=[tok(w) for w in widths],
        out_shape=[jax.ShapeDtypeStruct((bsz, seq, w), BF16) for w in widths],
        scratch_shapes=[pltpu.VMEM((t, d), BF16), pltpu.VMEM((8, d), F32)],
        compiler_params=pltpu.CompilerParams(
            dimension_semantics=("arbitrary", "arbitrary"),
            vmem_limit_bytes=VMEM_LIMIT_BYTES),
        name="mixer_project",
    )(h, shift, scale, norm_g, w_in, conv_w, cos, sin)


def _mixb_kernel(h_ref, g2_ref, qk_ref, v_ref, sg_ref, yb_ref, sq_ref, skv_ref, gt_ref,
                 intra_ref, xi_ref, zeta_ref, decay_ref, bias_ref, sink_ref,
                 pr_ref, pc_ref, ps_ref, wo_ref,
                 o_ref, state_ref, kvp_ref, ya_ref, yc_ref, *, tokens, chunk):
    d = D_MODEL
    first = pl.program_id(1) == 0

    @pl.when(first)
    def _():
        state_ref[...] = jnp.zeros_like(state_ref)
        kvp_ref[...] = jnp.zeros_like(kvp_ref)

    for c in range(tokens // chunk):
        rows = slice(c * chunk, (c + 1) * chunk)
        for hd in range(RET_HEADS):
            cols = slice(hd * RET_DIM, (hd + 1) * RET_DIM)
            q = qk_ref[rows, hd * RET_DIM:(hd + 1) * RET_DIM]
            k = qk_ref[rows, RET_WIDTH + hd * RET_DIM:RET_WIDTH + (hd + 1) * RET_DIM]
            v = v_ref[rows, cols]
            st = state_ref[hd]
            scores = _dot_nt(q, k) * intra_ref[hd]
            inner = _dot(scores.astype(BF16), v)
            cross = _dot(q, st.astype(BF16)) * xi_ref[hd]
            kz = (k.astype(F32) * zeta_ref[hd]).astype(BF16)
            state_ref[hd] = st * decay_ref[hd] + _dot_tn(kz, v)
            r = inner + cross
            rn = r * lax.rsqrt(jnp.mean(r * r, axis=-1, keepdims=True) + EPS)
            ya_ref[rows, cols] = (sg_ref[rows, cols].astype(F32) * rn).astype(BF16)

    lane = lax.broadcasted_iota(jnp.int32, (1, SWA_KV_W), 1)
    low = lane < SWA_DIM
    for n in range(tokens // SWA_BLOCK):
        rows = slice(n * SWA_BLOCK, (n + 1) * SWA_BLOCK)
        q = jnp.concatenate(
            [sq_ref[rows, g * SWA_KV_W:(g + 1) * SWA_KV_W] for g in range(SWA_GROUP)], axis=0)
        kv_cur = skv_ref[rows, :]
        if n == 0:
            kv_prev = kvp_ref[...].astype(BF16)
        else:
            kv_prev = skv_ref[(n - 1) * SWA_BLOCK:n * SWA_BLOCK, :]
        kband = jnp.concatenate([kv_prev[:, :SWA_KV_W], kv_cur[:, :SWA_KV_W]], axis=0)
        vband = jnp.concatenate([kv_prev[:, SWA_KV_W:], kv_cur[:, SWA_KV_W:]], axis=0)
        if n == 0:
            bias = bias_ref[jnp.where(first, 0, 1)]
        else:
            bias = bias_ref[1]
        outs = []
        for kh in range(SWA_KV_HEADS):
            keep = low if kh == 0 else jnp.logical_not(low)
            km = jnp.where(keep, kband, jnp.zeros_like(kband))
            s = _dot_nt(q, km) + bias
            sink = sink_ref[kh]
            m = jnp.maximum(jnp.max(s, axis=-1, keepdims=True), sink)
            p = jnp.exp(s - m)
            denom = jnp.sum(p, axis=-1, keepdims=True) + jnp.exp(sink - m)
            outs.append(_dot(p.astype(BF16), vband) / denom)
        o = jnp.where(low, outs[0], outs[1])
        for g in range(SWA_GROUP):
            yc_ref[rows, g * SWA_KV_W:(g + 1) * SWA_KV_W] = (
                o[g * SWA_BLOCK:(g + 1) * SWA_BLOCK, :].astype(BF16))
    kvp_ref[...] = skv_ref[tokens - SWA_BLOCK:, :].astype(F32)

    merged = gt_ref[:, 0:d].astype(F32) * _dot(ya_ref[...], pr_ref[...])
    merged += gt_ref[:, d:2 * d].astype(F32) * _dot(yb_ref[...], pc_ref[...])
    merged += gt_ref[:, 2 * d:3 * d].astype(F32) * _dot(yc_ref[...], ps_ref[...])
    o_ref[...] = h_ref[...] + g2_ref[...] * _dot(merged.astype(BF16), wo_ref[...])


def _mixer_combine(h, gate, qk, v, sg, yb, sq, skv, gt, tables, p_ret, p_conv, p_swa, w_out):
    bsz, seq, d = h.shape
    t = min(MIXB_TOKENS, seq)
    chunk = min(RET_CHUNK, t)
    intra, xi, zeta, decay, bias, sink = tables

    def tok(width):
        return pl.BlockSpec((None, t, width), lambda b, i: (b, i, 0))

    consts = [intra, xi, zeta, decay, bias, sink, p_ret, p_conv, p_swa, w_out]
    return pl.pallas_call(
        functools.partial(_mixb_kernel, tokens=t, chunk=chunk),
        grid=(bsz, seq // t),
        in_specs=[tok(d), _row_vec(d), tok(2 * RET_WIDTH), tok(RET_WIDTH), tok(RET_WIDTH),
                  tok(d), tok(d), tok(2 * SWA_KV_W), tok(3 * d)]
                 + [_resident(a.shape) for a in consts],
        out_specs=tok(d),
        out_shape=jax.ShapeDtypeStruct((bsz, seq, d), F32),
        scratch_shapes=[
            pltpu.VMEM((RET_HEADS, RET_DIM, RET_DIM), F32),
            pltpu.VMEM((SWA_BLOCK, 2 * SWA_KV_W), F32),
            pltpu.VMEM((t, d), BF16),
            pltpu.VMEM((t, d), BF16),
        ],
        compiler_params=pltpu.CompilerParams(
            dimension_semantics=("arbitrary", "arbitrary"),
            vmem_limit_bytes=VMEM_LIMIT_BYTES),
        name="mixer_combine",
    )(h, gate, qk, v, sg, yb, sq, skv, gt, *consts)


def _rope_tables(seq):
    inv = jnp.power(ROPE_BASE, -jnp.linspace(0.0, 1.0, ROPE_HALF, dtype=F32))
    ang = jnp.arange(seq, dtype=F32)[:, None] * inv[None, :]
    return jnp.cos(ang), jnp.sin(ang)


def _retention_tables(chunk):
    log_gamma = jnp.log1p(-jnp.exp2(-5.0 - jnp.arange(RET_HEADS, dtype=F32)))
    idx = jnp.arange(chunk, dtype=F32)
    rel = idx[:, None] - idx[None, :]
    intra = jnp.where(rel >= 0, jnp.exp(log_gamma[:, None, None] * jnp.maximum(rel, 0.0)), 0.0)
    xi = jnp.exp(log_gamma[:, None] * (idx + 1.0))
    zeta = jnp.exp(log_gamma[:, None] * (chunk - 1.0 - idx))
    decay = jnp.exp(log_gamma * chunk)
    wide = (RET_HEADS, chunk, RET_DIM)
    return (intra,
            jnp.broadcast_to(xi[:, :, None], wide),
            jnp.broadcast_to(zeta[:, :, None], wide),
            jnp.broadcast_to(decay[:, None, None], (RET_HEADS, 1, RET_DIM)))


def _swa_bias():
    w = SWA_BLOCK
    i = (jnp.arange(SWA_GROUP * w) % w)[:, None]
    j = jnp.arange(2 * w)[None, :]
    diff = i + w - j
    in_band = (diff >= 0) & (diff < w)
    neg = jnp.float32(-jnp.inf)
    general = jnp.where(in_band, 0.0, neg)
    first = jnp.where(in_band & (j >= w), 0.0, neg)
    return jnp.stack([first, general]).astype(F32)


def _swa_sinks(sinks):
    s = sinks.astype(F32).reshape(SWA_KV_HEADS, SWA_GROUP)
    return jnp.repeat(s, SWA_BLOCK, axis=1)[:, :, None]


def _group_major(w, axis):
    shape = w.shape
    split = shape[:axis] + (SWA_KV_HEADS, SWA_GROUP, SWA_DIM) + shape[axis + 1:]
    perm = list(range(len(split)))
    perm[axis], perm[axis + 1] = perm[axis + 1], perm[axis]
    return w.reshape(split).transpose(perm).reshape(shape)


def _ffn_weights(w1, w3, w2):
    d, dff = w1.shape
    n = dff // FFN_CHUNK
    w1c = w1.reshape(d, n, FFN_CHUNK).transpose(1, 0, 2)
    w3c = w3.reshape(d, n, FFN_CHUNK).transpose(1, 0, 2)
    w13 = jnp.concatenate([w1c, w3c], axis=-1).astype(BF16)
    return w13, w2.reshape(n, FFN_CHUNK, d).astype(BF16)


def kernel(x, c, ada_w, ada_b, ffn1_norm, ffn1_w1, ffn1_w3, ffn1_w2, mix_norm, w_in,
           conv_w, swa_sinks, p_ret, p_conv, p_swa, w_out, ffn2_norm, ffn2_w1, ffn2_w3,
           ffn2_w2, final_norm):
    bsz, seq, d = x.shape
    depth = ada_w.shape[0]
    mod = _modulation(c, ada_w, ada_b).reshape(depth, N_MOD, bsz, 1, d)
    cos, sin = _rope_tables(seq)
    ret_tables = _retention_tables(min(RET_CHUNK, MIXB_TOKENS, seq))
    bias = _swa_bias()

    h = x
    for l in range(depth):
        sh1, sc1, g1, sh2, sc2, g2, sh3, sc3, g3 = [mod[l, j] for j in range(N_MOD)]
        w13, w2 = _ffn_weights(ffn1_w1[l], ffn1_w3[l], ffn1_w2[l])
        h = _ffn(h, sh1, sc1, g1, ffn1_norm[l][None, :], w13, w2)

        wl = w_in[l]
        w_in_b = jnp.concatenate(
            [wl[:, :OFF_SQ], _group_major(wl[:, OFF_SQ:OFF_SKV], 1), wl[:, OFF_SKV:]],
            axis=1).astype(BF16)
        parts = _mixer_project(h, sh2, sc2, mix_norm[l][None, :], w_in_b, conv_w[l], cos, sin)
        tables = ret_tables + (bias, _swa_sinks(swa_sinks[l]))
        h = _mixer_combine(h, g2, *parts, tables,
                           p_ret[l].astype(BF16), p_conv[l].astype(BF16),
                           _group_major(p_swa[l], 0).astype(BF16), w_out[l].astype(BF16))

        w13, w2 = _ffn_weights(ffn2_w1[l], ffn2_w3[l], ffn2_w2[l])
        fin = final_norm[None, :] if l == depth - 1 else None
        h = _ffn(h, sh3, sc3, g3, ffn2_norm[l][None, :], w13, w2, final_g=fin)
    return h
```

```python
import functools

import jax
import jax.numpy as jnp
from jax import lax
from jax.experimental import pallas as pl
from jax.experimental.pallas import tpu as pltpu

F32 = jnp.float32
BF16 = jnp.bfloat16

D_MODEL = 1024
N_MOD = 9
EPS = 1e-6
MACARON_WEIGHT = 0.5

RET_HEADS = 4
RET_DIM = 256
RET_WIDTH = RET_HEADS * RET_DIM
ROPE_BASE = 10000.0
ROPE_HALF = RET_DIM // 2

CONV_K = 3

SWA_DIM = 64
SWA_Q_HEADS = 16
SWA_KV_HEADS = 2
SWA_GROUP = SWA_Q_HEADS // SWA_KV_HEADS
SWA_BLOCK = 128
SWA_KV_W = SWA_KV_HEADS * SWA_DIM

OFF_RQ = 0
OFF_RK = RET_WIDTH
OFF_RV = 2 * RET_WIDTH
OFF_RG = 3 * RET_WIDTH
OFF_CB = 4 * RET_WIDTH
OFF_CC = OFF_CB + D_MODEL
OFF_CX = OFF_CC + D_MODEL
OFF_SQ = OFF_CX + D_MODEL
OFF_SKV = OFF_SQ + SWA_Q_HEADS * SWA_DIM
OFF_GATE = OFF_SKV + 2 * SWA_KV_W
IN_COLS = OFF_GATE + 3 * D_MODEL

VMEM_LIMIT_BYTES = 56 * 1024 * 1024

FFN_TOKENS = 1024
FFN_CHUNK = 256
MIXA_TOKENS = 512
MIXB_TOKENS = 256
RET_CHUNK = 256


def _resident(shape):
    nd = len(shape)
    return pl.BlockSpec(shape, lambda *_: (0,) * nd, pipeline_mode=pl.Buffered(1))


def _row_vec(d):
    return pl.BlockSpec((None, 1, d), lambda b, i: (b, 0, 0))


def _norm_modulate(x, g, shift, scale):
    ms = jnp.mean(x * x, axis=-1, keepdims=True)
    n = x * lax.rsqrt(ms + EPS)
    return (n * g) * (1.0 + scale) + shift


def _silu(x):
    return x * jax.nn.sigmoid(x)


def _dot(a, b):
    return jnp.dot(a, b, preferred_element_type=F32)


def _dot_nt(a, b):
    return lax.dot_general(a, b, (((1,), (1,)), ((), ())), preferred_element_type=F32)


def _dot_tn(a, b):
    return lax.dot_general(a, b, (((0,), (0,)), ((), ())), preferred_element_type=F32)


def _mod_kernel(c_ref, w_ref, b_ref, o_ref):
    ca = _silu(c_ref[...]).astype(BF16)
    o_ref[...] = _dot(ca, w_ref[...].astype(BF16)) + b_ref[...]


def _modulation(c, ada_w, ada_b):
    depth = ada_w.shape[0]
    bsz, d = c.shape
    b4 = ada_b.reshape(depth, N_MOD, 1, d)
    return pl.pallas_call(
        _mod_kernel,
        grid=(depth, N_MOD),
        in_specs=[
            pl.BlockSpec((bsz, d), lambda l, j: (0, 0)),
            pl.BlockSpec((None, d, d), lambda l, j: (l, 0, j)),
            pl.BlockSpec((None, None, 1, d), lambda l, j: (l, j, 0, 0)),
        ],
        out_specs=pl.BlockSpec((None, None, bsz, d), lambda l, j: (l, j, 0, 0)),
        out_shape=jax.ShapeDtypeStruct((depth, N_MOD, bsz, d), F32),
        compiler_params=pltpu.CompilerParams(
            dimension_semantics=("arbitrary", "arbitrary")),
        name="adaln_modulation",
    )(c, ada_w, b4)


def _ffn_kernel(h_ref, sh_ref, sc_ref, gt_ref, g_ref, w1_ref, w3_ref, w2_ref, *rest,
                n_chunks, chunk, final):
    if final:
        fg_ref, o_ref, u_ref, acc_ref = rest
    else:
        o_ref, u_ref, acc_ref = rest
    x = h_ref[...]
    u_ref[...] = _norm_modulate(x, g_ref[...], sh_ref[...], sc_ref[...]).astype(BF16)

    for j in range(n_chunks):
        cols = slice(j * chunk, (j + 1) * chunk)
        a = _dot(u_ref[...], w1_ref[:, cols])
        b = _dot(u_ref[...], w3_ref[:, cols])
        act = (_silu(a) * b).astype(BF16)
        part = _dot(act, w2_ref[cols, :])
        if j == 0:
            acc_ref[...] = part
        else:
            acc_ref[...] += part

    y = x + (MACARON_WEIGHT * gt_ref[...]) * acc_ref[...]
    if final:
        ms = jnp.mean(y * y, axis=-1, keepdims=True)
        y = (y * lax.rsqrt(ms + EPS)) * fg_ref[...]
    o_ref[...] = y


def _ffn(h, shift, scale, gate, norm_g, w1, w3, w2, final_g=None):
    bsz, seq, d = h.shape
    dff = w1.shape[1]
    chunk = min(FFN_CHUNK, dff)
    tm = min(FFN_TOKENS, seq)
    final = final_g is not None
    in_specs = [
        pl.BlockSpec((None, tm, d), lambda b, i: (b, i, 0)),
        _row_vec(d), _row_vec(d), _row_vec(d),
        _resident((1, d)),
        _resident(w1.shape),
        _resident(w3.shape),
        _resident(w2.shape),
    ]
    args = [h, shift, scale, gate, norm_g, w1, w3, w2]
    if final:
        in_specs.append(_resident((1, d)))
        args.append(final_g)
    return pl.pallas_call(
        functools.partial(_ffn_kernel, n_chunks=dff // chunk, chunk=chunk, final=final),
        grid=(bsz, seq // tm),
        in_specs=in_specs,
        out_specs=pl.BlockSpec((None, tm, d), lambda b, i: (b, i, 0)),
        out_shape=jax.ShapeDtypeStruct((bsz, seq, d), F32),
        scratch_shapes=[pltpu.VMEM((tm, d), BF16), pltpu.VMEM((tm, d), F32)],
        compiler_params=pltpu.CompilerParams(
            dimension_semantics=("arbitrary", "arbitrary"),
            vmem_limit_bytes=VMEM_LIMIT_BYTES),
        name="swiglu_ffn_final" if final else "swiglu_ffn",
    )(*args)


def _mixa_kernel(h_ref, sh_ref, sc_ref, g_ref, w_ref, cw_ref, cos_ref, sin_ref,
                 qk_ref, v_ref, sg_ref, yb_ref, sq_ref, skv_ref, gt_ref,
                 u_ref, zc_ref, *, tokens):
    d = D_MODEL
    u_ref[...] = _norm_modulate(h_ref[...], g_ref[...], sh_ref[...], sc_ref[...]).astype(BF16)

    def proj(off, width):
        return _dot(u_ref[...], w_ref[:, off:off + width])

    cos = cos_ref[...]
    sin = sin_ref[...]
    for hd in range(2 * RET_HEADS):
        p = proj(hd * RET_DIM, RET_DIM)
        p1 = p[:, :ROPE_HALF]
        p2 = p[:, ROPE_HALF:]
        o1 = p1 * cos - p2 * sin
        o2 = p2 * cos + p1 * sin
        if hd >= RET_HEADS:
            o1 = o1 * (RET_DIM ** -0.5)
            o2 = o2 * (RET_DIM ** -0.5)
        qk_ref[:, hd * RET_DIM:hd * RET_DIM + ROPE_HALF] = o1.astype(BF16)
        qk_ref[:, hd * RET_DIM + ROPE_HALF:(hd + 1) * RET_DIM] = o2.astype(BF16)

    v_ref[...] = proj(OFF_RV, RET_WIDTH).astype(BF16)
    sg_ref[...] = _silu(proj(OFF_RG, RET_WIDTH)).astype(BF16)

    @pl.when(pl.program_id(1) == 0)
    def _():
        zc_ref[...] = jnp.zeros_like(zc_ref)

    z = proj(OFF_CC, d) * proj(OFF_CX, d)
    zcat = jnp.concatenate([zc_ref[...], z], axis=0)
    z1 = pltpu.roll(zcat, 1, axis=0)[8:]
    z2 = pltpu.roll(zcat, 2, axis=0)[8:]
    conv = z2 * cw_ref[0:1, :] + z1 * cw_ref[1:2, :] + z * cw_ref[2:3, :]
    zc_ref[...] = z[tokens - 8:]
    yb_ref[...] = (proj(OFF_CB, d) * conv).astype(BF16)

    sq_ref[...] = (proj(OFF_SQ, d) * (SWA_DIM ** -0.5)).astype(BF16)
    skv_ref[...] = proj(OFF_SKV, 2 * SWA_KV_W).astype(BF16)

    for j in range(3):
        gt_ref[:, j * d:(j + 1) * d] = jax.nn.sigmoid(proj(OFF_GATE + j * d, d)).astype(BF16)


def _mixer_project(h, shift, scale, norm_g, w_in, conv_w, cos, sin):
    bsz, seq, d = h.shape
    t = min(MIXA_TOKENS, seq)

    def tok(width):
        return pl.BlockSpec((None, t, width), lambda b, i: (b, i, 0))

    widths = [2 * RET_WIDTH, RET_WIDTH, RET_WIDTH, d, d, 2 * SWA_KV_W, 3 * d]
    return pl.pallas_call(
        functools.partial(_mixa_kernel, tokens=t),
        grid=(bsz, seq // t),
        in_specs=[
            tok(d), _row_vec(d), _row_vec(d),
            _resident((1, d)),
            _resident(w_in.shape),
            _resident(conv_w.shape),
            pl.BlockSpec((t, ROPE_HALF), lambda b, i: (i, 0)),
            pl.BlockSpec((t, ROPE_HALF), lambda b, i: (i, 0)),
        ],
        out_specs=[tok(w) for w in widths],
        out_shape=[jax.ShapeDtypeStruct((bsz, seq, w), BF16) for w in widths],
        scratch_shapes=[pltpu.VMEM((t, d), BF16), pltpu.VMEM((8, d), F32)],
        compiler_params=pltpu.CompilerParams(
            dimension_semantics=("arbitrary", "arbitrary"),
            vmem_limit_bytes=VMEM_LIMIT_BYTES),
        name="mixer_project",
    )(h, shift, scale, norm_g, w_in, conv_w, cos, sin)


def _mixb_kernel(h_ref, g2_ref, qk_ref, v_ref, sg_ref, yb_ref, sq_ref, skv_ref, gt_ref,
                 intra_ref, xi_ref, zeta_ref, decay_ref, bias_ref, sink_ref,
                 pr_ref, pc_ref, ps_ref, wo_ref,
                 o_ref, state_ref, kvp_ref, ya_ref, yc_ref, *, tokens, chunk):
    d = D_MODEL
    first = pl.program_id(1) == 0

    @pl.when(first)
    def _():
        state_ref[...] = jnp.zeros_like(state_ref)
        kvp_ref[...] = jnp.zeros_like(kvp_ref)

    for c in range(tokens // chunk):
        rows = slice(c * chunk, (c + 1) * chunk)
        for hd in range(RET_HEADS):
            cols = slice(hd * RET_DIM, (hd + 1) * RET_DIM)
            q = qk_ref[rows, hd * RET_DIM:(hd + 1) * RET_DIM]
            k = qk_ref[rows, RET_WIDTH + hd * RET_DIM:RET_WIDTH + (hd + 1) * RET_DIM]
            v = v_ref[rows, cols]
            st = state_ref[hd]
            scores = _dot_nt(q, k) * intra_ref[hd]
            inner = _dot(scores.astype(BF16), v)
            cross = _dot(q, st.astype(BF16)) * xi_ref[hd]
            kz = (k.astype(F32) * zeta_ref[hd]).astype(BF16)
            state_ref[hd] = st * decay_ref[hd] + _dot_tn(kz, v)
            r = inner + cross
            rn = r * lax.rsqrt(jnp.mean(r * r, axis=-1, keepdims=True) + EPS)
            ya_ref[rows, cols] = (sg_ref[rows, cols].astype(F32) * rn).astype(BF16)

    lane = lax.broadcasted_iota(jnp.int32, (1, SWA_KV_W), 1)
    low = lane < SWA_DIM
    for n in range(tokens // SWA_BLOCK):
        rows = slice(n * SWA_BLOCK, (n + 1) * SWA_BLOCK)
        q = jnp.concatenate(
            [sq_ref[rows, g * SWA_KV_W:(g + 1) * SWA_KV_W] for g in range(SWA_GROUP)], axis=0)
        kv_cur = skv_ref[rows, :]
        if n == 0:
            kv_prev = kvp_ref[...].astype(BF16)
        else:
            kv_prev = skv_ref[(n - 1) * SWA_BLOCK:n * SWA_BLOCK, :]
        kband = jnp.concatenate([kv_prev[:, :SWA_KV_W], kv_cur[:, :SWA_KV_W]], axis=0)
        vband = jnp.concatenate([kv_prev[:, SWA_KV_W:], kv_cur[:, SWA_KV_W:]], axis=0)
        if n == 0:
            bias = bias_ref[jnp.where(first, 0, 1)]
        else:
            bias = bias_ref[1]
        outs = []
        for kh in range(SWA_KV_HEADS):
            keep = low if kh == 0 else jnp.logical_not(low)
            km = jnp.where(keep, kband, jnp.zeros_like(kband))
            s = _dot_nt(q, km) + bias
            sink = sink_ref[kh]
            m = jnp.maximum(jnp.max(s, axis=-1, keepdims=True), sink)
            p = jnp.exp(s - m)
            denom = jnp.sum(p, axis=-1, keepdims=True) + jnp.exp(sink - m)
            outs.append(_dot(p.astype(BF16), vband) / denom)
        o = jnp.where(low, outs[0], outs[1])
        for g in range(SWA_GROUP):
            yc_ref[rows, g * SWA_KV_W:(g + 1) * SWA_KV_W] = (
                o[g * SWA_BLOCK:(g + 1) * SWA_BLOCK, :].astype(BF16))
    kvp_ref[...] = skv_ref[tokens - SWA_BLOCK:, :].astype(F32)

    merged = gt_ref[:, 0:d].astype(F32) * _dot(ya_ref[...], pr_ref[...])
    merged += gt_ref[:, d:2 * d].astype(F32) * _dot(yb_ref[...], pc_ref[...])
    merged += gt_ref[:, 2 * d:3 * d].astype(F32) * _dot(yc_ref[...], ps_ref[...])
    o_ref[...] = h_ref[...] + g2_ref[...] * _dot(merged.astype(BF16), wo_ref[...])


def _mixer_combine(h, gate, qk, v, sg, yb, sq, skv, gt, tables, p_ret, p_conv, p_swa, w_out):
    bsz, seq, d = h.shape
    t = min(MIXB_TOKENS, seq)
    chunk = min(RET_CHUNK, t)
    intra, xi, zeta, decay, bias, sink = tables

    def tok(width):
        return pl.BlockSpec((None, t, width), lambda b, i: (b, i, 0))

    consts = [intra, xi, zeta, decay, bias, sink, p_ret, p_conv, p_swa, w_out]
    return pl.pallas_call(
        functools.partial(_mixb_kernel, tokens=t, chunk=chunk),
        grid=(bsz, seq // t),
        in_specs=[tok(d), _row_vec(d), tok(2 * RET_WIDTH), tok(RET_WIDTH), tok(RET_WIDTH),
                  tok(d), tok(d), tok(2 * SWA_KV_W), tok(3 * d)]
                 + [_resident(a.shape) for a in consts],
        out_specs=tok(d),
        out_shape=jax.ShapeDtypeStruct((bsz, seq, d), F32),
        scratch_shapes=[
            pltpu.VMEM((RET_HEADS, RET_DIM, RET_DIM), F32),
            pltpu.VMEM((SWA_BLOCK, 2 * SWA_KV_W), F32),
            pltpu.VMEM((t, d), BF16),
            pltpu.VMEM((t, d), BF16),
        ],
        compiler_params=pltpu.CompilerParams(
            dimension_semantics=("arbitrary", "arbitrary"),
            vmem_limit_bytes=VMEM_LIMIT_BYTES),
        name="mixer_combine",
    )(h, gate, qk, v, sg, yb, sq, skv, gt, *consts)


def _rope_tables(seq):
    inv = jnp.power(ROPE_BASE, -jnp.linspace(0.0, 1.0, ROPE_HALF, dtype=F32))
    ang = jnp.arange(seq, dtype=F32)[:, None] * inv[None, :]
    return jnp.cos(ang), jnp.sin(ang)


def _retention_tables(chunk):
    log_gamma = jnp.log1p(-jnp.exp2(-5.0 - jnp.arange(RET_HEADS, dtype=F32)))
    idx = jnp.arange(chunk, dtype=F32)
    rel = idx[:, None] - idx[None, :]
    intra = jnp.where(rel >= 0, jnp.exp(log_gamma[:, None, None] * jnp.maximum(rel, 0.0)), 0.0)
    xi = jnp.exp(log_gamma[:, None] * (idx + 1.0))
    zeta = jnp.exp(log_gamma[:, None] * (chunk - 1.0 - idx))
    decay = jnp.exp(log_gamma * chunk)
    wide = (RET_HEADS, chunk, RET_DIM)
    return (intra,
            jnp.broadcast_to(xi[:, :, None], wide),
            jnp.broadcast_to(zeta[:, :, None], wide),
            jnp.broadcast_to(decay[:, None, None], (RET_HEADS, 1, RET_DIM)))


def _swa_bias():
    w = SWA_BLOCK
    i = (jnp.arange(SWA_GROUP * w) % w)[:, None]
    j = jnp.arange(2 * w)[None, :]
    diff = i + w - j
    in_band = (diff >= 0) & (diff < w)
    neg = jnp.float32(-jnp.inf)
    general = jnp.where(in_band, 0.0, neg)
    first = jnp.where(in_band & (j >= w), 0.0, neg)
    return jnp.stack([first, general]).astype(F32)


def _swa_sinks(sinks):
    s = sinks.astype(F32).reshape(SWA_KV_HEADS, SWA_GROUP)
    return jnp.repeat(s, SWA_BLOCK, axis=1)[:, :, None]


def _group_major(w, axis):
    shape = w.shape
    split = shape[:axis] + (SWA_KV_HEADS, SWA_GROUP, SWA_DIM) + shape[axis + 1:]
    perm = list(range(len(split)))
    perm[axis], perm[axis + 1] = perm[axis + 1], perm[axis]
    return w.reshape(split).transpose(perm).reshape(shape)


def kernel(x, c, ada_w, ada_b, ffn1_norm, ffn1_w1, ffn1_w3, ffn1_w2, mix_norm, w_in,
           conv_w, swa_sinks, p_ret, p_conv, p_swa, w_out, ffn2_norm, ffn2_w1, ffn2_w3,
           ffn2_w2, final_norm):
    bsz, seq, d = x.shape
    depth = ada_w.shape[0]
    mod = _modulation(c, ada_w, ada_b).reshape(depth, N_MOD, bsz, 1, d)
    cos, sin = _rope_tables(seq)
    ret_tables = _retention_tables(min(RET_CHUNK, MIXB_TOKENS, seq))
    bias = _swa_bias()

    h = x
    for l in range(depth):
        sh1, sc1, g1, sh2, sc2, g2, sh3, sc3, g3 = [mod[l, j] for j in range(N_MOD)]
        h = _ffn(h, sh1, sc1, g1, ffn1_norm[l][None, :], ffn1_w1[l].astype(BF16),
                 ffn1_w3[l].astype(BF16), ffn1_w2[l].astype(BF16))

        wl = w_in[l]
        w_in_b = jnp.concatenate(
            [wl[:, :OFF_SQ], _group_major(wl[:, OFF_SQ:OFF_SKV], 1), wl[:, OFF_SKV:]],
            axis=1).astype(BF16)
        parts = _mixer_project(h, sh2, sc2, mix_norm[l][None, :], w_in_b, conv_w[l], cos, sin)
        tables = ret_tables + (bias, _swa_sinks(swa_sinks[l]))
        h = _mixer_combine(h, g2, *parts, tables,
                           p_ret[l].astype(BF16), p_conv[l].astype(BF16),
                           _group_major(p_swa[l], 0).astype(BF16), w_out[l].astype(BF16))

        fin = final_norm[None, :] if l == depth - 1 else None
        h = _ffn(h, sh3, sc3, g3, ffn2_norm[l][None, :], ffn2_w1[l].astype(BF16),
                 ffn2_w3[l].astype(BF16), ffn2_w2[l].astype(BF16), final_g=fin)
    return h
```

```python
import functools

import jax
import jax.numpy as jnp
from jax import lax
from jax.experimental import pallas as pl
from jax.experimental.pallas import tpu as pltpu

F32 = jnp.float32
BF16 = jnp.bfloat16

D_MODEL = 1024
N_MOD = 9
EPS = 1e-6
MACARON_WEIGHT = 0.5

RET_HEADS = 4
RET_DIM = 256
RET_WIDTH = RET_HEADS * RET_DIM
ROPE_BASE = 10000.0
ROPE_HALF = RET_DIM // 2

CONV_K = 3

SWA_DIM = 64
SWA_Q_HEADS = 16
SWA_KV_HEADS = 2
SWA_GROUP = SWA_Q_HEADS // SWA_KV_HEADS
SWA_BLOCK = 128
SWA_KV_W = SWA_KV_HEADS * SWA_DIM

OFF_RQ = 0
OFF_RK = RET_WIDTH
OFF_RV = 2 * RET_WIDTH
OFF_RG = 3 * RET_WIDTH
OFF_CB = 4 * RET_WIDTH
OFF_CC = OFF_CB + D_MODEL
OFF_CX = OFF_CC + D_MODEL
OFF_SQ = OFF_CX + D_MODEL
OFF_SKV = OFF_SQ + SWA_Q_HEADS * SWA_DIM
OFF_GATE = OFF_SKV + 2 * SWA_KV_W
IN_COLS = OFF_GATE + 3 * D_MODEL

VMEM_LIMIT_BYTES = 56 * 1024 * 1024

FFN_TOKENS = 1024
FFN_CHUNK = 256
MIX_TOKENS = 256
RET_CHUNK = 256


def _resident(shape):
    nd = len(shape)
    return pl.BlockSpec(shape, lambda *_: (0,) * nd, pipeline_mode=pl.Buffered(1))


def _row_vec(d):
    return pl.BlockSpec((None, 1, d), lambda b, i: (b, 0, 0))


def _norm_modulate(x, g, shift, scale):
    ms = jnp.mean(x * x, axis=-1, keepdims=True)
    n = x * lax.rsqrt(ms + EPS)
    return (n * g) * (1.0 + scale) + shift


def _silu(x):
    return x * jax.nn.sigmoid(x)


def _dot(a, b):
    return jnp.dot(a, b, preferred_element_type=F32)


def _dot_nt(a, b):
    return lax.dot_general(a, b, (((1,), (1,)), ((), ())), preferred_element_type=F32)


def _dot_tn(a, b):
    return lax.dot_general(a, b, (((0,), (0,)), ((), ())), preferred_element_type=F32)


def _mod_kernel(c_ref, w_ref, b_ref, o_ref):
    ca = _silu(c_ref[...]).astype(BF16)
    o_ref[...] = _dot(ca, w_ref[...].astype(BF16)) + b_ref[...]


def _modulation(c, ada_w, ada_b):
    depth = ada_w.shape[0]
    bsz, d = c.shape
    b4 = ada_b.reshape(depth, N_MOD, 1, d)
    return pl.pallas_call(
        _mod_kernel,
        grid=(depth, N_MOD),
        in_specs=[
            pl.BlockSpec((bsz, d), lambda l, j: (0, 0)),
            pl.BlockSpec((None, d, d), lambda l, j: (l, 0, j)),
            pl.BlockSpec((None, None, 1, d), lambda l, j: (l, j, 0, 0)),
        ],
        out_specs=pl.BlockSpec((None, None, bsz, d), lambda l, j: (l, j, 0, 0)),
        out_shape=jax.ShapeDtypeStruct((depth, N_MOD, bsz, d), F32),
        compiler_params=pltpu.CompilerParams(
            dimension_semantics=("arbitrary", "arbitrary")),
        name="adaln_modulation",
    )(c, ada_w, b4)


def _ffn_kernel(h_ref, sh_ref, sc_ref, gt_ref, g_ref, w1_ref, w3_ref, w2_ref, *rest,
                n_chunks, chunk, final):
    if final:
        fg_ref, o_ref, u_ref, acc_ref = rest
    else:
        o_ref, u_ref, acc_ref = rest
    x = h_ref[...]
    u_ref[...] = _norm_modulate(x, g_ref[...], sh_ref[...], sc_ref[...]).astype(BF16)

    for j in range(n_chunks):
        cols = slice(j * chunk, (j + 1) * chunk)
        a = _dot(u_ref[...], w1_ref[:, cols])
        b = _dot(u_ref[...], w3_ref[:, cols])
        act = (_silu(a) * b).astype(BF16)
        part = _dot(act, w2_ref[cols, :])
        if j == 0:
            acc_ref[...] = part
        else:
            acc_ref[...] += part

    y = x + (MACARON_WEIGHT * gt_ref[...]) * acc_ref[...]
    if final:
        ms = jnp.mean(y * y, axis=-1, keepdims=True)
        y = (y * lax.rsqrt(ms + EPS)) * fg_ref[...]
    o_ref[...] = y


def _ffn(h, shift, scale, gate, norm_g, w1, w3, w2, final_g=None):
    bsz, seq, d = h.shape
    dff = w1.shape[1]
    chunk = min(FFN_CHUNK, dff)
    tm = min(FFN_TOKENS, seq)
    final = final_g is not None
    in_specs = [
        pl.BlockSpec((None, tm, d), lambda b, i: (b, i, 0)),
        _row_vec(d), _row_vec(d), _row_vec(d),
        _resident((1, d)),
        _resident(w1.shape),
        _resident(w3.shape),
        _resident(w2.shape),
    ]
    args = [h, shift, scale, gate, norm_g, w1, w3, w2]
    if final:
        in_specs.append(_resident((1, d)))
        args.append(final_g)
    return pl.pallas_call(
        functools.partial(_ffn_kernel, n_chunks=dff // chunk, chunk=chunk, final=final),
        grid=(bsz, seq // tm),
        in_specs=in_specs,
        out_specs=pl.BlockSpec((None, tm, d), lambda b, i: (b, i, 0)),
        out_shape=jax.ShapeDtypeStruct((bsz, seq, d), F32),
        scratch_shapes=[pltpu.VMEM((tm, d), BF16), pltpu.VMEM((tm, d), F32)],
        compiler_params=pltpu.CompilerParams(
            dimension_semantics=("arbitrary", "arbitrary"),
            vmem_limit_bytes=VMEM_LIMIT_BYTES),
        name="swiglu_ffn_final" if final else "swiglu_ffn",
    )(*args)


PROJ_CHUNK = 256
ACTIVE_CHAINS = 4


def _interleave(chains, dense):
    chains = list(chains)
    dense = list(dense)
    stages_left = sum(c[2] for c in chains)
    issued = 0
    active = []
    done = set()
    while chains or active or dense:
        ready = [c for c in chains if c[1] <= issued]
        for c in ready[:ACTIVE_CHAINS - len(active)]:
            chains.remove(c)
            active.append(c)
        for c in list(active):
            stages_left -= 1
            if next(c[3], "done") == "done":
                active.remove(c)
                done.add(c[0])
        ticks_left = max(1, -(-stages_left // ACTIVE_CHAINS))
        budget = -(-len(dense) // ticks_left)
        if not active and chains:
            budget = max(budget, min(c[1] for c in chains) - issued)
        for _ in range(budget):
            item = next((it for it in dense if it[1] <= done), None)
            if item is None:
                break
            dense.remove(item)
            item[0]()
            issued += 1


def _mixer_kernel(h_ref, sh_ref, sc_ref, g2_ref, g_ref, w_ref, cw_ref, cos_ref, sin_ref,
                  intra_ref, xi_ref, zeta_ref, decay_ref, bias_ref, sink_ref,
                  pr_ref, pc_ref, ps_ref, wo_ref, o_ref,
                  u_ref, zc_ref, qk_ref, v_ref, sg_ref, yb_ref, sq_ref, skv_ref, gt_ref,
                  state_ref, kvp_ref, ya_ref, yc_ref, mg_ref, *, tokens, chunk):
    d = D_MODEL
    pc = PROJ_CHUNK
    first = pl.program_id(1) == 0

    @pl.when(first)
    def _():
        zc_ref[...] = jnp.zeros_like(zc_ref)
        state_ref[...] = jnp.zeros_like(state_ref)
        kvp_ref[...] = jnp.zeros_like(kvp_ref)

    u_ref[...] = _norm_modulate(h_ref[...], g_ref[...], sh_ref[...], sc_ref[...]).astype(BF16)

    def proj(off, width=pc):
        return _dot(u_ref[...], w_ref[:, off:off + width])

    def attn_kv():
        skv_ref[...] = proj(OFF_SKV, 2 * SWA_KV_W).astype(BF16)

    def attn_q(j):
        sq_ref[:, j * pc:(j + 1) * pc] = (proj(OFF_SQ + j * pc) * (SWA_DIM ** -0.5)).astype(BF16)

    def rotary(hd):
        p = proj(hd * RET_DIM, RET_DIM)
        cos = cos_ref[...]
        sin = sin_ref[...]
        p1 = p[:, :ROPE_HALF]
        p2 = p[:, ROPE_HALF:]
        o1 = p1 * cos - p2 * sin
        o2 = p2 * cos + p1 * sin
        if hd >= RET_HEADS:
            o1 = o1 * (RET_DIM ** -0.5)
            o2 = o2 * (RET_DIM ** -0.5)
        qk_ref[:, hd * RET_DIM:hd * RET_DIM + ROPE_HALF] = o1.astype(BF16)
        qk_ref[:, hd * RET_DIM + ROPE_HALF:(hd + 1) * RET_DIM] = o2.astype(BF16)

    def ret_v(j):
        v_ref[:, j * pc:(j + 1) * pc] = proj(OFF_RV + j * pc).astype(BF16)

    def ret_gate(j):
        sg_ref[:, j * pc:(j + 1) * pc] = _silu(proj(OFF_RG + j * pc)).astype(BF16)

    def conv(j):
        cols = slice(j * pc, (j + 1) * pc)
        z = proj(OFF_CC + j * pc) * proj(OFF_CX + j * pc)
        zcat = jnp.concatenate([zc_ref[:, cols], z], axis=0)
        z1 = pltpu.roll(zcat, 1, axis=0)[8:]
        z2 = pltpu.roll(zcat, 2, axis=0)[8:]
        y = z2 * cw_ref[0:1, cols] + z1 * cw_ref[1:2, cols] + z * cw_ref[2:3, cols]
        zc_ref[:, cols] = z[tokens - 8:]
        yb_ref[:, cols] = (proj(OFF_CB + j * pc) * y).astype(BF16)

    def merge_gate(j):
        gt_ref[:, j * pc:(j + 1) * pc] = jax.nn.sigmoid(proj(OFF_GATE + j * pc))

    def merge_conv(j):
        cols = slice(j * pc, (j + 1) * pc)
        mg_ref[:, cols] = gt_ref[:, d + j * pc:d + (j + 1) * pc] * _dot(yb_ref[...], pc_ref[:, cols])

    def merge_attention(j):
        cols = slice(j * pc, (j + 1) * pc)
        mg_ref[:, cols] += (gt_ref[:, 2 * d + j * pc:2 * d + (j + 1) * pc]
                            * _dot(yc_ref[...], ps_ref[:, cols]))

    def merge_retention(hd):
        hrows = slice(hd * RET_DIM, (hd + 1) * RET_DIM)
        mg_ref[...] += gt_ref[:, 0:d] * _dot(ya_ref[:, hrows], pr_ref[hrows, :])

    def retention(c, hd):
        cols = slice(hd * RET_DIM, (hd + 1) * RET_DIM)
        crows = slice(c * chunk, (c + 1) * chunk)
        k = qk_ref[crows, RET_WIDTH + hd * RET_DIM:RET_WIDTH + (hd + 1) * RET_DIM]
        v = v_ref[crows, cols]
        st = state_ref[hd]
        stb = st.astype(BF16)
        half = chunk // 2
        scores, cross = [], []
        for i in range(2):
            rows = slice(c * chunk + i * half, c * chunk + (i + 1) * half)
            sub = slice(i * half, (i + 1) * half)
            q = qk_ref[rows, hd * RET_DIM:(hd + 1) * RET_DIM]
            scores.append((_dot_nt(q, k) * intra_ref[hd, sub, :]).astype(BF16))
            cross.append(_dot(q, stb) * xi_ref[hd, sub, :])
        yield
        kz = (k.astype(F32) * zeta_ref[hd]).astype(BF16)
        state_ref[hd] = st * decay_ref[hd] + _dot_tn(kz, v)
        for i in range(2):
            rows = slice(c * chunk + i * half, c * chunk + (i + 1) * half)
            r = _dot(scores[i], v) + cross[i]
            rn = r * lax.rsqrt(jnp.mean(r * r, axis=-1, keepdims=True) + EPS)
            ya_ref[rows, cols] = (sg_ref[rows, cols].astype(F32) * rn).astype(BF16)

    lane = lax.broadcasted_iota(jnp.int32, (1, SWA_KV_W), 1)
    low = lane < SWA_DIM

    def attention(n, g):
        rows = slice(n * SWA_BLOCK, (n + 1) * SWA_BLOCK)
        q = sq_ref[rows, g * SWA_KV_W:(g + 1) * SWA_KV_W]
        kv_cur = skv_ref[rows, :]
        if n == 0:
            kv_prev = kvp_ref[...].astype(BF16)
            bias = bias_ref[jnp.where(first, 0, 1)]
        else:
            kv_prev = skv_ref[(n - 1) * SWA_BLOCK:n * SWA_BLOCK, :]
            bias = bias_ref[1]
        kband = jnp.concatenate([kv_prev[:, :SWA_KV_W], kv_cur[:, :SWA_KV_W]], axis=0)
        vband = jnp.concatenate([kv_prev[:, SWA_KV_W:], kv_cur[:, SWA_KV_W:]], axis=0)
        s = []
        for kh in range(SWA_KV_HEADS):
            keep = low if kh == 0 else jnp.logical_not(low)
            km = jnp.where(keep, kband, jnp.zeros_like(kband))
            s.append(_dot_nt(q, km) + bias)
        yield
        p, den = [], []
        for kh in range(SWA_KV_HEADS):
            sink = sink_ref[kh, g]
            m = jnp.maximum(jnp.max(s[kh], axis=-1, keepdims=True), sink)
            e = jnp.exp(s[kh] - m)
            den.append(jnp.sum(e, axis=-1, keepdims=True) + jnp.exp(sink - m))
            p.append(e.astype(BF16))
        yield
        o = [_dot(p[kh], vband) / den[kh] for kh in range(SWA_KV_HEADS)]
        yc_ref[rows, g * SWA_KV_W:(g + 1) * SWA_KV_W] = jnp.where(low, o[0], o[1]).astype(BF16)

    attn_kv()
    for j in range(d // pc):
        attn_q(j)
    nq = d // pc
    n_blocks = tokens // SWA_BLOCK
    n_chunks = tokens // chunk
    attn = [(("attn", n, g), 0, 3, attention(n, g))
            for n in range(n_blocks) for g in range(SWA_GROUP)]
    ret = [(("ret", c, hd), 4 * (hd + 1), 2, retention(c, hd))
           for c in range(n_chunks) for hd in range(RET_HEADS)]
    free = frozenset()
    dense = []
    for hd in range(RET_HEADS):
        dense += [(functools.partial(rotary, hd), free),
                  (functools.partial(rotary, RET_HEADS + hd), free),
                  (functools.partial(ret_v, hd), free), (functools.partial(ret_gate, hd), free)]
    for j in range(nq):
        dense += [(functools.partial(conv, j), free), (functools.partial(merge_gate, nq + j), free)]
    dense += [(functools.partial(merge_conv, j), free) for j in range(nq)]
    dense += [(functools.partial(merge_gate, j), free) for j in range(nq)]
    dense += [(functools.partial(merge_gate, 2 * nq + j), free) for j in range(nq)]
    all_attn = frozenset(c[0] for c in attn)
    dense += [(functools.partial(merge_attention, j), all_attn) for j in range(nq)]
    for hd in range(RET_HEADS):
        dense.append((functools.partial(merge_retention, hd),
                      frozenset(("ret", c, hd) for c in range(n_chunks))))
    _interleave(attn + ret, dense)
    kvp_ref[...] = skv_ref[tokens - SWA_BLOCK:, :].astype(F32)

    o_ref[...] = h_ref[...] + g2_ref[...] * _dot(mg_ref[...].astype(BF16), wo_ref[...])


def _mixer(h, shift, scale, gate, norm_g, w_in, conv_w, cos, sin, tables,
           p_ret, p_conv, p_swa, w_out):
    bsz, seq, d = h.shape
    t = min(MIX_TOKENS, seq)
    chunk = min(RET_CHUNK, t)
    consts = [norm_g, w_in, conv_w]
    tail = list(tables) + [p_ret, p_conv, p_swa, w_out]
    tok = pl.BlockSpec((None, t, d), lambda b, i: (b, i, 0))
    rope = pl.BlockSpec((t, ROPE_HALF), lambda b, i: (i, 0))
    return pl.pallas_call(
        functools.partial(_mixer_kernel, tokens=t, chunk=chunk),
        grid=(bsz, seq // t),
        in_specs=[tok, _row_vec(d), _row_vec(d), _row_vec(d)]
                 + [_resident(a.shape) for a in consts] + [rope, rope]
                 + [_resident(a.shape) for a in tail[:5]]
                 + [pl.BlockSpec(memory_space=pltpu.SMEM)]
                 + [_resident(a.shape) for a in tail[6:]],
        out_specs=tok,
        out_shape=jax.ShapeDtypeStruct((bsz, seq, d), F32),
        scratch_shapes=[
            pltpu.VMEM((t, d), BF16),
            pltpu.VMEM((8, d), F32),
            pltpu.VMEM((t, 2 * RET_WIDTH), BF16),
            pltpu.VMEM((t, RET_WIDTH), BF16),
            pltpu.VMEM((t, RET_WIDTH), BF16),
            pltpu.VMEM((t, d), BF16),
            pltpu.VMEM((t, d), BF16),
            pltpu.VMEM((t, 2 * SWA_KV_W), BF16),
            pltpu.VMEM((t, 3 * d), F32),
            pltpu.VMEM((RET_HEADS, RET_DIM, RET_DIM), F32),
            pltpu.VMEM((SWA_BLOCK, 2 * SWA_KV_W), F32),
            pltpu.VMEM((t, d), BF16),
            pltpu.VMEM((t, d), BF16),
            pltpu.VMEM((t, d), F32),
        ],
        compiler_params=pltpu.CompilerParams(
            dimension_semantics=("arbitrary", "arbitrary"),
            vmem_limit_bytes=VMEM_LIMIT_BYTES),
        name="hybrid_mixer",
    )(h, shift, scale, gate, *consts, cos, sin, *tail)


def _rope_tables(seq):
    inv = jnp.power(ROPE_BASE, -jnp.linspace(0.0, 1.0, ROPE_HALF, dtype=F32))
    ang = jnp.arange(seq, dtype=F32)[:, None] * inv[None, :]
    return jnp.cos(ang), jnp.sin(ang)


def _retention_tables(chunk):
    log_gamma = jnp.log1p(-jnp.exp2(-5.0 - jnp.arange(RET_HEADS, dtype=F32)))
    idx = jnp.arange(chunk, dtype=F32)
    rel = idx[:, None] - idx[None, :]
    intra = jnp.where(rel >= 0, jnp.exp(log_gamma[:, None, None] * jnp.maximum(rel, 0.0)), 0.0)
    xi = jnp.exp(log_gamma[:, None] * (idx + 1.0))
    zeta = jnp.exp(log_gamma[:, None] * (chunk - 1.0 - idx))
    decay = jnp.exp(log_gamma * chunk)
    wide = (RET_HEADS, chunk, RET_DIM)
    return (intra,
            jnp.broadcast_to(xi[:, :, None], wide),
            jnp.broadcast_to(zeta[:, :, None], wide),
            jnp.broadcast_to(decay[:, None, None], (RET_HEADS, 1, RET_DIM)))


def _swa_bias():
    w = SWA_BLOCK
    i = jnp.arange(w)[:, None]
    j = jnp.arange(2 * w)[None, :]
    diff = i + w - j
    in_band = (diff >= 0) & (diff < w)
    neg = jnp.float32(-jnp.inf)
    general = jnp.where(in_band, 0.0, neg)
    first = jnp.where(in_band & (j >= w), 0.0, neg)
    return jnp.stack([first, general]).astype(F32)


def _group_major(w, axis):
    shape = w.shape
    split = shape[:axis] + (SWA_KV_HEADS, SWA_GROUP, SWA_DIM) + shape[axis + 1:]
    perm = list(range(len(split)))
    perm[axis], perm[axis + 1] = perm[axis + 1], perm[axis]
    return w.reshape(split).transpose(perm).reshape(shape)


def kernel(x, c, ada_w, ada_b, ffn1_norm, ffn1_w1, ffn1_w3, ffn1_w2, mix_norm, w_in,
           conv_w, swa_sinks, p_ret, p_conv, p_swa, w_out, ffn2_norm, ffn2_w1, ffn2_w3,
           ffn2_w2, final_norm):
    bsz, seq, d = x.shape
    depth = ada_w.shape[0]
    mod = _modulation(c, ada_w, ada_b).reshape(depth, N_MOD, bsz, 1, d)
    cos, sin = _rope_tables(seq)
    ret_tables = _retention_tables(min(RET_CHUNK, MIX_TOKENS, seq))
    bias = _swa_bias()

    h = x
    for l in range(depth):
        sh1, sc1, g1, sh2, sc2, g2, sh3, sc3, g3 = [mod[l, j] for j in range(N_MOD)]
        h = _ffn(h, sh1, sc1, g1, ffn1_norm[l][None, :], ffn1_w1[l].astype(BF16),
                 ffn1_w3[l].astype(BF16), ffn1_w2[l].astype(BF16))

        wl = w_in[l]
        w_in_b = jnp.concatenate(
            [wl[:, :OFF_SQ], _group_major(wl[:, OFF_SQ:OFF_SKV], 1), wl[:, OFF_SKV:]],
            axis=1).astype(BF16)
        tables = ret_tables + (bias, swa_sinks[l].astype(F32).reshape(SWA_KV_HEADS, SWA_GROUP))
        h = _mixer(h, sh2, sc2, g2, mix_norm[l][None, :], w_in_b, conv_w[l], cos, sin, tables,
                   p_ret[l].astype(BF16), p_conv[l].astype(BF16),
                   _group_major(p_swa[l], 0).astype(BF16), w_out[l].astype(BF16))

        fin = final_norm[None, :] if l == depth - 1 else None
        h = _ffn(h, sh3, sc3, g3, ffn2_norm[l][None, :], ffn2_w1[l].astype(BF16),
                 ffn2_w3[l].astype(BF16), ffn2_w2[l].astype(BF16), final_g=fin)
    return h
```

```python
import functools

import jax
import jax.numpy as jnp
from jax import lax
from jax.experimental import pallas as pl
from jax.experimental.pallas import tpu as pltpu

F32 = jnp.float32
BF16 = jnp.bfloat16

D_MODEL = 1024
N_MOD = 9
EPS = 1e-6
MACARON_WEIGHT = 0.5

RET_HEADS = 4
RET_DIM = 256
RET_WIDTH = RET_HEADS * RET_DIM
ROPE_BASE = 10000.0
ROPE_HALF = RET_DIM // 2

CONV_K = 3

SWA_DIM = 64
SWA_Q_HEADS = 16
SWA_KV_HEADS = 2
SWA_GROUP = SWA_Q_HEADS // SWA_KV_HEADS
SWA_BLOCK = 128
SWA_KV_W = SWA_KV_HEADS * SWA_DIM

OFF_RQ = 0
OFF_RK = RET_WIDTH
OFF_RV = 2 * RET_WIDTH
OFF_RG = 3 * RET_WIDTH
OFF_CB = 4 * RET_WIDTH
OFF_CC = OFF_CB + D_MODEL
OFF_CX = OFF_CC + D_MODEL
OFF_SQ = OFF_CX + D_MODEL
OFF_SKV = OFF_SQ + SWA_Q_HEADS * SWA_DIM
OFF_GATE = OFF_SKV + 2 * SWA_KV_W
IN_COLS = OFF_GATE + 3 * D_MODEL

VMEM_LIMIT_BYTES = 56 * 1024 * 1024

FFN_TOKENS = 1024
FFN_CHUNK = 256
MIX_TOKENS = 256
RET_CHUNK = 256


def _resident(shape):
    nd = len(shape)
    return pl.BlockSpec(shape, lambda *_: (0,) * nd, pipeline_mode=pl.Buffered(1))


def _resident_layer(stacked_shape, layer):
    nd = len(stacked_shape) - 1
    return pl.BlockSpec((None,) + tuple(stacked_shape[1:]), lambda *_: (layer,) + (0,) * nd,
                        pipeline_mode=pl.Buffered(1))


def _row_vec(d):
    return pl.BlockSpec((None, 1, d), lambda b, i: (b, 0, 0))


def _norm_modulate(x, g, shift, scale):
    ms = jnp.mean(x * x, axis=-1, keepdims=True)
    n = x * lax.rsqrt(ms + EPS)
    return (n * g) * (1.0 + scale) + shift


def _silu(x):
    return x * jax.nn.sigmoid(x)


def _dot(a, b):
    return jnp.dot(a, b, preferred_element_type=F32)


def _dot_nt(a, b):
    return lax.dot_general(a, b, (((1,), (1,)), ((), ())), preferred_element_type=F32)


def _dot_tn(a, b):
    return lax.dot_general(a, b, (((0,), (0,)), ((), ())), preferred_element_type=F32)


def _mod_kernel(c_ref, w_ref, b_ref, o_ref):
    ca = _silu(c_ref[...]).astype(BF16)
    o_ref[...] = _dot(ca, w_ref[...].astype(BF16)) + b_ref[...]


def _modulation(c, ada_w, ada_b):
    depth = ada_w.shape[0]
    bsz, d = c.shape
    b4 = ada_b.reshape(depth, N_MOD, 1, d)
    return pl.pallas_call(
        _mod_kernel,
        grid=(depth, N_MOD),
        in_specs=[
            pl.BlockSpec((bsz, d), lambda l, j: (0, 0)),
            pl.BlockSpec((None, d, d), lambda l, j: (l, 0, j)),
            pl.BlockSpec((None, None, 1, d), lambda l, j: (l, j, 0, 0)),
        ],
        out_specs=pl.BlockSpec((None, None, bsz, d), lambda l, j: (l, j, 0, 0)),
        out_shape=jax.ShapeDtypeStruct((depth, N_MOD, bsz, d), F32),
        compiler_params=pltpu.CompilerParams(
            dimension_semantics=("arbitrary", "arbitrary")),
        name="adaln_modulation",
    )(c, ada_w, b4)


def _ffn_kernel(h_ref, sh_ref, sc_ref, gt_ref, g_ref, w1_ref, w3_ref, w2_ref, *rest,
                n_chunks, chunk, final):
    if final:
        fg_ref, o_ref, u_ref, acc_ref = rest
    else:
        o_ref, u_ref, acc_ref = rest
    x = h_ref[...]
    u_ref[...] = _norm_modulate(x, g_ref[...], sh_ref[...], sc_ref[...]).astype(BF16)

    for j in range(n_chunks):
        cols = slice(j * chunk, (j + 1) * chunk)
        a = _dot(u_ref[...], w1_ref[:, cols])
        b = _dot(u_ref[...], w3_ref[:, cols])
        act = (_silu(a) * b).astype(BF16)
        part = _dot(act, w2_ref[cols, :])
        if j == 0:
            acc_ref[...] = part
        else:
            acc_ref[...] += part

    y = x + (MACARON_WEIGHT * gt_ref[...]) * acc_ref[...]
    if final:
        ms = jnp.mean(y * y, axis=-1, keepdims=True)
        y = (y * lax.rsqrt(ms + EPS)) * fg_ref[...]
    o_ref[...] = y


def _ffn(h, shift, scale, gate, norm_g, w1, w3, w2, layer, final_g=None):
    bsz, seq, d = h.shape
    dff = w1.shape[2]
    chunk = min(FFN_CHUNK, dff)
    tm = min(FFN_TOKENS, seq)
    final = final_g is not None
    in_specs = [
        pl.BlockSpec((None, tm, d), lambda b, i: (b, i, 0)),
        _row_vec(d), _row_vec(d), _row_vec(d),
        _resident((1, d)),
        _resident_layer(w1.shape, layer),
        _resident_layer(w3.shape, layer),
        _resident_layer(w2.shape, layer),
    ]
    args = [h, shift, scale, gate, norm_g, w1, w3, w2]
    if final:
        in_specs.append(_resident((1, d)))
        args.append(final_g)
    return pl.pallas_call(
        functools.partial(_ffn_kernel, n_chunks=dff // chunk, chunk=chunk, final=final),
        grid=(bsz, seq // tm),
        in_specs=in_specs,
        out_specs=pl.BlockSpec((None, tm, d), lambda b, i: (b, i, 0)),
        out_shape=jax.ShapeDtypeStruct((bsz, seq, d), F32),
        scratch_shapes=[pltpu.VMEM((tm, d), BF16), pltpu.VMEM((tm, d), F32)],
        compiler_params=pltpu.CompilerParams(
            dimension_semantics=("arbitrary", "arbitrary"),
            vmem_limit_bytes=VMEM_LIMIT_BYTES),
        name="swiglu_ffn_final" if final else "swiglu_ffn",
    )(*args)


PROJ_CHUNK = 256
ACTIVE_CHAINS = 4


def _interleave(chains, dense):
    chains = list(chains)
    dense = list(dense)
    stages_left = sum(c[2] for c in chains)
    issued = 0
    active = []
    done = set()
    while chains or active or dense:
        ready = [c for c in chains if c[1] <= issued]
        for c in ready[:ACTIVE_CHAINS - len(active)]:
            chains.remove(c)
            active.append(c)
        for c in list(active):
            stages_left -= 1
            if next(c[3], "done") == "done":
                active.remove(c)
                done.add(c[0])
        ticks_left = max(1, -(-stages_left // ACTIVE_CHAINS))
        budget = -(-len(dense) // ticks_left)
        if not active and chains:
            budget = max(budget, min(c[1] for c in chains) - issued)
        for _ in range(budget):
            item = next((it for it in dense if it[1] <= done), None)
            if item is None:
                break
            dense.remove(item)
            item[0]()
            issued += 1


def _mixer_kernel(h_ref, sh_ref, sc_ref, g2_ref, g_ref, w_ref, cw_ref, cos_ref, sin_ref,
                  intra_ref, xi_ref, zeta_ref, decay_ref, bias_ref, sink_ref,
                  pr_ref, pc_ref, ps_ref, wo_ref, o_ref,
                  u_ref, zc_ref, qk_ref, v_ref, sg_ref, yb_ref, sq_ref, skv_ref, gt_ref,
                  state_ref, kvp_ref, ya_ref, yc_ref, mg_ref, *, tokens, chunk):
    d = D_MODEL
    pc = PROJ_CHUNK
    first = pl.program_id(1) == 0

    @pl.when(first)
    def _():
        zc_ref[...] = jnp.zeros_like(zc_ref)
        state_ref[...] = jnp.zeros_like(state_ref)
        kvp_ref[...] = jnp.zeros_like(kvp_ref)

    u_ref[...] = _norm_modulate(h_ref[...], g_ref[...], sh_ref[...], sc_ref[...]).astype(BF16)

    def proj(off, width=pc):
        return _dot(u_ref[...], w_ref[:, off:off + width])

    def attn_kv():
        skv_ref[...] = proj(OFF_SKV, 2 * SWA_KV_W).astype(BF16)

    def attn_q(j):
        sq_ref[:, j * pc:(j + 1) * pc] = (proj(OFF_SQ + j * pc) * (SWA_DIM ** -0.5)).astype(BF16)

    def rotary(hd):
        p = proj(hd * RET_DIM, RET_DIM)
        cos = cos_ref[...]
        sin = sin_ref[...]
        p1 = p[:, :ROPE_HALF]
        p2 = p[:, ROPE_HALF:]
        o1 = p1 * cos - p2 * sin
        o2 = p2 * cos + p1 * sin
        if hd >= RET_HEADS:
            o1 = o1 * (RET_DIM ** -0.5)
            o2 = o2 * (RET_DIM ** -0.5)
        qk_ref[:, hd * RET_DIM:hd * RET_DIM + ROPE_HALF] = o1.astype(BF16)
        qk_ref[:, hd * RET_DIM + ROPE_HALF:(hd + 1) * RET_DIM] = o2.astype(BF16)

    def ret_v(j):
        v_ref[:, j * pc:(j + 1) * pc] = proj(OFF_RV + j * pc).astype(BF16)

    def ret_gate(j):
        sg_ref[:, j * pc:(j + 1) * pc] = _silu(proj(OFF_RG + j * pc)).astype(BF16)

    def conv(j):
        cols = slice(j * pc, (j + 1) * pc)
        z = proj(OFF_CC + j * pc) * proj(OFF_CX + j * pc)
        zcat = jnp.concatenate([zc_ref[:, cols], z], axis=0)
        z1 = pltpu.roll(zcat, 1, axis=0)[8:]
        z2 = pltpu.roll(zcat, 2, axis=0)[8:]
        y = z2 * cw_ref[0:1, cols] + z1 * cw_ref[1:2, cols] + z * cw_ref[2:3, cols]
        zc_ref[:, cols] = z[tokens - 8:]
        yb_ref[:, cols] = (proj(OFF_CB + j * pc) * y).astype(BF16)

    def merge_gate(j):
        gt_ref[:, j * pc:(j + 1) * pc] = jax.nn.sigmoid(proj(OFF_GATE + j * pc))

    def merge_conv(j):
        cols = slice(j * pc, (j + 1) * pc)
        mg_ref[:, cols] = gt_ref[:, d + j * pc:d + (j + 1) * pc] * _dot(yb_ref[...], pc_ref[:, cols])

    def merge_attention(j):
        cols = slice(j * pc, (j + 1) * pc)
        mg_ref[:, cols] += (gt_ref[:, 2 * d + j * pc:2 * d + (j + 1) * pc]
                            * _dot(yc_ref[...], ps_ref[:, cols]))

    def merge_retention(hd):
        hrows = slice(hd * RET_DIM, (hd + 1) * RET_DIM)
        mg_ref[...] += gt_ref[:, 0:d] * _dot(ya_ref[:, hrows], pr_ref[hrows, :])

    def retention(c, hd):
        cols = slice(hd * RET_DIM, (hd + 1) * RET_DIM)
        crows = slice(c * chunk, (c + 1) * chunk)
        k = qk_ref[crows, RET_WIDTH + hd * RET_DIM:RET_WIDTH + (hd + 1) * RET_DIM]
        v = v_ref[crows, cols]
        st = state_ref[hd]
        stb = st.astype(BF16)
        half = chunk // 2
        scores, cross = [], []
        for i in range(2):
            rows = slice(c * chunk + i * half, c * chunk + (i + 1) * half)
            sub = slice(i * half, (i + 1) * half)
            q = qk_ref[rows, hd * RET_DIM:(hd + 1) * RET_DIM]
            scores.append((_dot_nt(q, k) * intra_ref[hd, sub, :]).astype(BF16))
            cross.append(_dot(q, stb) * xi_ref[hd, sub, :])
        yield
        kz = (k.astype(F32) * zeta_ref[hd]).astype(BF16)
        state_ref[hd] = st * decay_ref[hd] + _dot_tn(kz, v)
        for i in range(2):
            rows = slice(c * chunk + i * half, c * chunk + (i + 1) * half)
            r = _dot(scores[i], v) + cross[i]
            rn = r * lax.rsqrt(jnp.mean(r * r, axis=-1, keepdims=True) + EPS)
            ya_ref[rows, cols] = (sg_ref[rows, cols].astype(F32) * rn).astype(BF16)

    lane = lax.broadcasted_iota(jnp.int32, (1, SWA_KV_W), 1)
    low = lane < SWA_DIM

    def attention(n, g):
        rows = slice(n * SWA_BLOCK, (n + 1) * SWA_BLOCK)
        q = sq_ref[rows, g * SWA_KV_W:(g + 1) * SWA_KV_W]
        kv_cur = skv_ref[rows, :]
        if n == 0:
            kv_prev = kvp_ref[...].astype(BF16)
            bias = bias_ref[jnp.where(first, 0, 1)]
        else:
            kv_prev = skv_ref[(n - 1) * SWA_BLOCK:n * SWA_BLOCK, :]
            bias = bias_ref[1]
        kband = jnp.concatenate([kv_prev[:, :SWA_KV_W], kv_cur[:, :SWA_KV_W]], axis=0)
        vband = jnp.concatenate([kv_prev[:, SWA_KV_W:], kv_cur[:, SWA_KV_W:]], axis=0)
        s = []
        for kh in range(SWA_KV_HEADS):
            keep = low if kh == 0 else jnp.logical_not(low)
            km = jnp.where(keep, kband, jnp.zeros_like(kband))
            s.append(_dot_nt(q, km) + bias)
        yield
        p, den = [], []
        for kh in range(SWA_KV_HEADS):
            sink = sink_ref[kh, g]
            m = jnp.maximum(jnp.max(s[kh], axis=-1, keepdims=True), sink)
            e = jnp.exp(s[kh] - m)
            den.append(jnp.sum(e, axis=-1, keepdims=True) + jnp.exp(sink - m))
            p.append(e.astype(BF16))
        yield
        o = [_dot(p[kh], vband) / den[kh] for kh in range(SWA_KV_HEADS)]
        yc_ref[rows, g * SWA_KV_W:(g + 1) * SWA_KV_W] = jnp.where(low, o[0], o[1]).astype(BF16)

    attn_kv()
    for j in range(d // pc):
        attn_q(j)
    nq = d // pc
    n_blocks = tokens // SWA_BLOCK
    n_chunks = tokens // chunk
    attn = [(("attn", n, g), 0, 3, attention(n, g))
            for n in range(n_blocks) for g in range(SWA_GROUP)]
    ret = [(("ret", c, hd), 4 * (hd + 1), 2, retention(c, hd))
           for c in range(n_chunks) for hd in range(RET_HEADS)]
    free = frozenset()
    dense = []
    for hd in range(RET_HEADS):
        dense += [(functools.partial(rotary, hd), free),
                  (functools.partial(rotary, RET_HEADS + hd), free),
                  (functools.partial(ret_v, hd), free), (functools.partial(ret_gate, hd), free)]
    for j in range(nq):
        dense += [(functools.partial(conv, j), free), (functools.partial(merge_gate, nq + j), free)]
    _interleave(attn + ret, dense)
    kvp_ref[...] = skv_ref[tokens - SWA_BLOCK:, :].astype(F32)

    for j in range(nq):
        merge_gate(j)
        merge_gate(2 * nq + j)
    for j in range(nq):
        merge_conv(j)
    for j in range(nq):
        merge_attention(j)
    for hd in range(RET_HEADS):
        merge_retention(hd)

    o_ref[...] = h_ref[...] + g2_ref[...] * _dot(mg_ref[...].astype(BF16), wo_ref[...])


def _mixer(h, shift, scale, gate, norm_g, w_in, conv_w, cos, sin, tables,
           p_ret, p_conv, p_swa, w_out, layer):
    bsz, seq, d = h.shape
    t = min(MIX_TOKENS, seq)
    chunk = min(RET_CHUNK, t)
    tok = pl.BlockSpec((None, t, d), lambda b, i: (b, i, 0))
    rope = pl.BlockSpec((t, ROPE_HALF), lambda b, i: (i, 0))
    intra, xi, zeta, decay, bias, sinks = tables
    small = [intra, xi, zeta, decay, bias]
    stacked = [p_ret, p_conv, p_swa, w_out]
    return pl.pallas_call(
        functools.partial(_mixer_kernel, tokens=t, chunk=chunk),
        grid=(bsz, seq // t),
        in_specs=[tok, _row_vec(d), _row_vec(d), _row_vec(d),
                  _resident(norm_g.shape), _resident_layer(w_in.shape, layer),
                  _resident(conv_w.shape), rope, rope]
                 + [_resident(a.shape) for a in small]
                 + [pl.BlockSpec(memory_space=pltpu.SMEM)]
                 + [_resident_layer(a.shape, layer) for a in stacked],
        out_specs=tok,
        out_shape=jax.ShapeDtypeStruct((bsz, seq, d), F32),
        scratch_shapes=[
            pltpu.VMEM((t, d), BF16),
            pltpu.VMEM((8, d), F32),
            pltpu.VMEM((t, 2 * RET_WIDTH), BF16),
            pltpu.VMEM((t, RET_WIDTH), BF16),
            pltpu.VMEM((t, RET_WIDTH), BF16),
            pltpu.VMEM((t, d), BF16),
            pltpu.VMEM((t, d), BF16),
            pltpu.VMEM((t, 2 * SWA_KV_W), BF16),
            pltpu.VMEM((t, 3 * d), F32),
            pltpu.VMEM((RET_HEADS, RET_DIM, RET_DIM), F32),
            pltpu.VMEM((SWA_BLOCK, 2 * SWA_KV_W), F32),
            pltpu.VMEM((t, d), BF16),
            pltpu.VMEM((t, d), BF16),
            pltpu.VMEM((t, d), F32),
        ],
        compiler_params=pltpu.CompilerParams(
            dimension_semantics=("arbitrary", "arbitrary"),
            vmem_limit_bytes=VMEM_LIMIT_BYTES),
        name="hybrid_mixer",
    )(h, shift, scale, gate, norm_g, w_in, conv_w, cos, sin, *small, sinks, *stacked)


def _rope_tables(seq):
    inv = jnp.power(ROPE_BASE, -jnp.linspace(0.0, 1.0, ROPE_HALF, dtype=F32))
    ang = jnp.arange(seq, dtype=F32)[:, None] * inv[None, :]
    return jnp.cos(ang), jnp.sin(ang)


def _retention_tables(chunk):
    log_gamma = jnp.log1p(-jnp.exp2(-5.0 - jnp.arange(RET_HEADS, dtype=F32)))
    idx = jnp.arange(chunk, dtype=F32)
    rel = idx[:, None] - idx[None, :]
    intra = jnp.where(rel >= 0, jnp.exp(log_gamma[:, None, None] * jnp.maximum(rel, 0.0)), 0.0)
    xi = jnp.exp(log_gamma[:, None] * (idx + 1.0))
    zeta = jnp.exp(log_gamma[:, None] * (chunk - 1.0 - idx))
    decay = jnp.exp(log_gamma * chunk)
    wide = (RET_HEADS, chunk, RET_DIM)
    return (intra,
            jnp.broadcast_to(xi[:, :, None], wide),
            jnp.broadcast_to(zeta[:, :, None], wide),
            jnp.broadcast_to(decay[:, None, None], (RET_HEADS, 1, RET_DIM)))


def _swa_bias():
    w = SWA_BLOCK
    i = jnp.arange(w)[:, None]
    j = jnp.arange(2 * w)[None, :]
    diff = i + w - j
    in_band = (diff >= 0) & (diff < w)
    neg = jnp.float32(-jnp.inf)
    general = jnp.where(in_band, 0.0, neg)
    first = jnp.where(in_band & (j >= w), 0.0, neg)
    return jnp.stack([first, general]).astype(F32)


def _group_major(w, axis):
    shape = w.shape
    split = shape[:axis] + (SWA_KV_HEADS, SWA_GROUP, SWA_DIM) + shape[axis + 1:]
    perm = list(range(len(split)))
    perm[axis], perm[axis + 1] = perm[axis + 1], perm[axis]
    return w.reshape(split).transpose(perm).reshape(shape)


def kernel(x, c, ada_w, ada_b, ffn1_norm, ffn1_w1, ffn1_w3, ffn1_w2, mix_norm, w_in,
           conv_w, swa_sinks, p_ret, p_conv, p_swa, w_out, ffn2_norm, ffn2_w1, ffn2_w3,
           ffn2_w2, final_norm):
    bsz, seq, d = x.shape
    depth = ada_w.shape[0]
    mod = _modulation(c, ada_w, ada_b).reshape(depth, N_MOD, bsz, 1, d)
    cos, sin = _rope_tables(seq)
    ret_tables = _retention_tables(min(RET_CHUNK, MIX_TOKENS, seq))
    bias = _swa_bias()

    ffn1 = [w.astype(BF16) for w in (ffn1_w1, ffn1_w3, ffn1_w2)]
    ffn2 = [w.astype(BF16) for w in (ffn2_w1, ffn2_w3, ffn2_w2)]
    w_in_b = jnp.concatenate(
        [w_in[:, :, :OFF_SQ], _group_major(w_in[:, :, OFF_SQ:OFF_SKV], 2), w_in[:, :, OFF_SKV:]],
        axis=2).astype(BF16)
    outs = [p_ret.astype(BF16), p_conv.astype(BF16), _group_major(p_swa, 1).astype(BF16),
            w_out.astype(BF16)]
    sinks = swa_sinks.astype(F32).reshape(depth, SWA_KV_HEADS, SWA_GROUP)

    h = x
    for l in range(depth):
        sh1, sc1, g1, sh2, sc2, g2, sh3, sc3, g3 = [mod[l, j] for j in range(N_MOD)]
        h = _ffn(h, sh1, sc1, g1, ffn1_norm[l][None, :], *ffn1, l)
        h = _mixer(h, sh2, sc2, g2, mix_norm[l][None, :], w_in_b, conv_w[l], cos, sin,
                   ret_tables + (bias, sinks[l]), *outs, l)
        fin = final_norm[None, :] if l == depth - 1 else None
        h = _ffn(h, sh3, sc3, g3, ffn2_norm[l][None, :], *ffn2, l, final_g=fin)
    return h
```

```python
import functools

import jax
import jax.numpy as jnp
from jax import lax
from jax.experimental import pallas as pl
from jax.experimental.pallas import tpu as pltpu

F32 = jnp.float32
BF16 = jnp.bfloat16

D_MODEL = 1024
N_MOD = 9
EPS = 1e-6
MACARON_WEIGHT = 0.5

RET_HEADS = 4
RET_DIM = 256
RET_WIDTH = RET_HEADS * RET_DIM
ROPE_BASE = 10000.0
ROPE_HALF = RET_DIM // 2

CONV_K = 3

SWA_DIM = 64
SWA_Q_HEADS = 16
SWA_KV_HEADS = 2
SWA_GROUP = SWA_Q_HEADS // SWA_KV_HEADS
SWA_BLOCK = 128
SWA_KV_W = SWA_KV_HEADS * SWA_DIM

OFF_RQ = 0
OFF_RK = RET_WIDTH
OFF_RV = 2 * RET_WIDTH
OFF_RG = 3 * RET_WIDTH
OFF_CB = 4 * RET_WIDTH
OFF_CC = OFF_CB + D_MODEL
OFF_CX = OFF_CC + D_MODEL
OFF_SQ = OFF_CX + D_MODEL
OFF_SKV = OFF_SQ + SWA_Q_HEADS * SWA_DIM
OFF_GATE = OFF_SKV + 2 * SWA_KV_W
IN_COLS = OFF_GATE + 3 * D_MODEL

VMEM_LIMIT_BYTES = 56 * 1024 * 1024

FFN_TOKENS = 1024
FFN_CHUNK = 256
MIX_TOKENS = 256
RET_CHUNK = 256


def _resident(shape):
    nd = len(shape)
    return pl.BlockSpec(shape, lambda *_: (0,) * nd, pipeline_mode=pl.Buffered(1))


def _resident_layer(stacked_shape, layer):
    nd = len(stacked_shape) - 1
    return pl.BlockSpec((None,) + tuple(stacked_shape[1:]), lambda *_: (layer,) + (0,) * nd,
                        pipeline_mode=pl.Buffered(1))


def _row_vec(d):
    return pl.BlockSpec((None, 1, d), lambda b, i: (b, 0, 0))


def _norm_modulate(x, g, shift, scale):
    ms = jnp.mean(x * x, axis=-1, keepdims=True)
    n = x * lax.rsqrt(ms + EPS)
    return (n * g) * (1.0 + scale) + shift


def _silu(x):
    return x * jax.nn.sigmoid(x)


def _dot(a, b):
    return jnp.dot(a, b, preferred_element_type=F32)


def _dot_nt(a, b):
    return lax.dot_general(a, b, (((1,), (1,)), ((), ())), preferred_element_type=F32)


def _dot_tn(a, b):
    return lax.dot_general(a, b, (((0,), (0,)), ((), ())), preferred_element_type=F32)


def _mod_kernel(c_ref, w_ref, b_ref, o_ref):
    ca = _silu(c_ref[...]).astype(BF16)
    o_ref[...] = _dot(ca, w_ref[...].astype(BF16)) + b_ref[...]


def _modulation(c, ada_w, ada_b):
    depth = ada_w.shape[0]
    bsz, d = c.shape
    b4 = ada_b.reshape(depth, N_MOD, 1, d)
    return pl.pallas_call(
        _mod_kernel,
        grid=(depth, N_MOD),
        in_specs=[
            pl.BlockSpec((bsz, d), lambda l, j: (0, 0)),
            pl.BlockSpec((None, d, d), lambda l, j: (l, 0, j)),
            pl.BlockSpec((None, None, 1, d), lambda l, j: (l, j, 0, 0)),
        ],
        out_specs=pl.BlockSpec((None, None, bsz, d), lambda l, j: (l, j, 0, 0)),
        out_shape=jax.ShapeDtypeStruct((depth, N_MOD, bsz, d), F32),
        compiler_params=pltpu.CompilerParams(
            dimension_semantics=("arbitrary", "arbitrary")),
        name="adaln_modulation",
    )(c, ada_w, b4)


def _ffn_kernel(h_ref, sh_ref, sc_ref, gt_ref, g_ref, w1_ref, w3_ref, w2_ref, *rest,
                n_chunks, chunk, final):
    if final:
        fg_ref, o_ref, u_ref, acc_ref = rest
    else:
        o_ref, u_ref, acc_ref = rest
    x = h_ref[...]
    u_ref[...] = _norm_modulate(x, g_ref[...], sh_ref[...], sc_ref[...]).astype(BF16)

    for j in range(n_chunks):
        cols = slice(j * chunk, (j + 1) * chunk)
        a = _dot(u_ref[...], w1_ref[:, cols])
        b = _dot(u_ref[...], w3_ref[:, cols])
        act = (_silu(a) * b).astype(BF16)
        part = _dot(act, w2_ref[cols, :])
        if j == 0:
            acc_ref[...] = part
        else:
            acc_ref[...] += part

    y = x + (MACARON_WEIGHT * gt_ref[...]) * acc_ref[...]
    if final:
        ms = jnp.mean(y * y, axis=-1, keepdims=True)
        y = (y * lax.rsqrt(ms + EPS)) * fg_ref[...]
    o_ref[...] = y


def _ffn(h, shift, scale, gate, norm_g, w1, w3, w2, layer, final_g=None):
    bsz, seq, d = h.shape
    dff = w1.shape[2]
    chunk = min(FFN_CHUNK, dff)
    tm = min(FFN_TOKENS, seq)
    final = final_g is not None
    in_specs = [
        pl.BlockSpec((None, tm, d), lambda b, i: (b, i, 0)),
        _row_vec(d), _row_vec(d), _row_vec(d),
        _resident((1, d)),
        _resident_layer(w1.shape, layer),
        _resident_layer(w3.shape, layer),
        _resident_layer(w2.shape, layer),
    ]
    args = [h, shift, scale, gate, norm_g, w1, w3, w2]
    if final:
        in_specs.append(_resident((1, d)))
        args.append(final_g)
    return pl.pallas_call(
        functools.partial(_ffn_kernel, n_chunks=dff // chunk, chunk=chunk, final=final),
        grid=(bsz, seq // tm),
        in_specs=in_specs,
        out_specs=pl.BlockSpec((None, tm, d), lambda b, i: (b, i, 0)),
        out_shape=jax.ShapeDtypeStruct((bsz, seq, d), F32),
        scratch_shapes=[pltpu.VMEM((tm, d), BF16), pltpu.VMEM((tm, d), F32)],
        compiler_params=pltpu.CompilerParams(
            dimension_semantics=("arbitrary", "arbitrary"),
            vmem_limit_bytes=VMEM_LIMIT_BYTES),
        name="swiglu_ffn_final" if final else "swiglu_ffn",
    )(*args)


PROJ_CHUNK = 256
ACTIVE_CHAINS = 4


def _interleave(chains, dense):
    chains = list(chains)
    dense = list(dense)
    stages_left = sum(c[2] for c in chains)
    issued = 0
    active = []
    done = set()
    while chains or active or dense:
        ready = [c for c in chains if c[1] <= issued]
        for c in ready[:ACTIVE_CHAINS - len(active)]:
            chains.remove(c)
            active.append(c)
        for c in list(active):
            stages_left -= 1
            if next(c[3], "done") == "done":
                active.remove(c)
                done.add(c[0])
        ticks_left = max(1, -(-stages_left // ACTIVE_CHAINS))
        budget = -(-len(dense) // ticks_left)
        if not active and chains:
            budget = max(budget, min(c[1] for c in chains) - issued)
        for _ in range(budget):
            item = next((it for it in dense if it[1] <= done), None)
            if item is None:
                break
            dense.remove(item)
            item[0]()
            issued += 1


def _mixer_kernel(h_ref, sh_ref, sc_ref, g2_ref, g_ref, w_ref, wq_ref, cw_ref, cos_ref, sin_ref,
                  intra_ref, xi_ref, zeta_ref, decay_ref, bias_ref, sink_ref,
                  pr_ref, pc_ref, ps_ref, wo_ref, o_ref,
                  u_ref, zc_ref, qk_ref, v_ref, sg_ref, yb_ref, sq_ref, skv_ref,
                  state_ref, kvp_ref, ya_ref, yc_ref, mg_ref, *, tokens, chunk):
    d = D_MODEL
    pc = PROJ_CHUNK
    first = pl.program_id(1) == 0

    @pl.when(first)
    def _():
        zc_ref[...] = jnp.zeros_like(zc_ref)
        state_ref[...] = jnp.zeros_like(state_ref)
        kvp_ref[...] = jnp.zeros_like(kvp_ref)

    u_ref[...] = _norm_modulate(h_ref[...], g_ref[...], sh_ref[...], sc_ref[...]).astype(BF16)

    def proj(off, width=pc):
        return _dot(u_ref[...], w_ref[:, off:off + width])

    def attn_kv():
        skv_ref[...] = proj(OFF_SKV, 2 * SWA_KV_W).astype(BF16)

    def attn_q(j):
        q = _dot(u_ref[...], wq_ref[:, j * pc:(j + 1) * pc])
        sq_ref[:, j * pc:(j + 1) * pc] = (q * (SWA_DIM ** -0.5)).astype(BF16)

    def rotary(hd):
        p = proj(hd * RET_DIM, RET_DIM)
        cos = cos_ref[...]
        sin = sin_ref[...]
        p1 = p[:, :ROPE_HALF]
        p2 = p[:, ROPE_HALF:]
        o1 = p1 * cos - p2 * sin
        o2 = p2 * cos + p1 * sin
        if hd >= RET_HEADS:
            o1 = o1 * (RET_DIM ** -0.5)
            o2 = o2 * (RET_DIM ** -0.5)
        qk_ref[:, hd * RET_DIM:hd * RET_DIM + ROPE_HALF] = o1.astype(BF16)
        qk_ref[:, hd * RET_DIM + ROPE_HALF:(hd + 1) * RET_DIM] = o2.astype(BF16)

    def ret_v(j):
        v_ref[:, j * pc:(j + 1) * pc] = proj(OFF_RV + j * pc).astype(BF16)

    def ret_gate(j):
        sg_ref[:, j * pc:(j + 1) * pc] = _silu(proj(OFF_RG + j * pc)).astype(BF16)

    def conv(j):
        cols = slice(j * pc, (j + 1) * pc)
        z = proj(OFF_CC + j * pc) * proj(OFF_CX + j * pc)
        zcat = jnp.concatenate([zc_ref[:, cols], z], axis=0)
        z1 = pltpu.roll(zcat, 1, axis=0)[8:]
        z2 = pltpu.roll(zcat, 2, axis=0)[8:]
        y = z2 * cw_ref[0:1, cols] + z1 * cw_ref[1:2, cols] + z * cw_ref[2:3, cols]
        zc_ref[:, cols] = z[tokens - 8:]
        yb_ref[:, cols] = (proj(OFF_CB + j * pc) * y).astype(BF16)

    def merge(j):
        cols = slice(j * pc, (j + 1) * pc)

        def gate(branch):
            return jax.nn.sigmoid(proj(OFF_GATE + branch * d + j * pc))

        acc = gate(1) * _dot(yb_ref[...], pc_ref[:, cols])
        acc += gate(2) * _dot(yc_ref[...], ps_ref[:, cols])
        acc += gate(0) * _dot(ya_ref[...], pr_ref[:, cols])
        mg_ref[:, cols] = acc.astype(BF16)

    def retention(c, hd):
        cols = slice(hd * RET_DIM, (hd + 1) * RET_DIM)
        crows = slice(c * chunk, (c + 1) * chunk)
        k = qk_ref[crows, RET_WIDTH + hd * RET_DIM:RET_WIDTH + (hd + 1) * RET_DIM]
        v = v_ref[crows, cols]
        st = state_ref[hd]
        stb = st.astype(BF16)
        half = chunk // 2
        scores, cross = [], []
        for i in range(2):
            rows = slice(c * chunk + i * half, c * chunk + (i + 1) * half)
            sub = slice(i * half, (i + 1) * half)
            q = qk_ref[rows, hd * RET_DIM:(hd + 1) * RET_DIM]
            scores.append((_dot_nt(q, k) * intra_ref[hd, sub, :]).astype(BF16))
            cross.append(_dot(q, stb) * xi_ref[hd, sub, :])
        yield
        kz = (k.astype(F32) * zeta_ref[hd]).astype(BF16)
        state_ref[hd] = st * decay_ref[hd] + _dot_tn(kz, v)
        for i in range(2):
            rows = slice(c * chunk + i * half, c * chunk + (i + 1) * half)
            r = _dot(scores[i], v) + cross[i]
            rn = r * lax.rsqrt(jnp.mean(r * r, axis=-1, keepdims=True) + EPS)
            ya_ref[rows, cols] = (sg_ref[rows, cols].astype(F32) * rn).astype(BF16)

    lane = lax.broadcasted_iota(jnp.int32, (1, SWA_KV_W), 1)
    low = lane < SWA_DIM

    def attention(n, g):
        rows = slice(n * SWA_BLOCK, (n + 1) * SWA_BLOCK)
        q = sq_ref[rows, g * SWA_KV_W:(g + 1) * SWA_KV_W]
        kv_cur = skv_ref[rows, :]
        if n == 0:
            kv_prev = kvp_ref[...].astype(BF16)
            bias = bias_ref[jnp.where(first, 0, 1)]
        else:
            kv_prev = skv_ref[(n - 1) * SWA_BLOCK:n * SWA_BLOCK, :]
            bias = bias_ref[1]
        kband = jnp.concatenate([kv_prev[:, :SWA_KV_W], kv_cur[:, :SWA_KV_W]], axis=0)
        vband = jnp.concatenate([kv_prev[:, SWA_KV_W:], kv_cur[:, SWA_KV_W:]], axis=0)
        s = []
        for kh in range(SWA_KV_HEADS):
            keep = low if kh == 0 else jnp.logical_not(low)
            km = jnp.where(keep, kband, jnp.zeros_like(kband))
            s.append(_dot_nt(q, km) + bias)
        yield
        p, den = [], []
        for kh in range(SWA_KV_HEADS):
            sink = sink_ref[kh, g]
            m = jnp.maximum(jnp.max(s[kh], axis=-1, keepdims=True), sink)
            e = jnp.exp(s[kh] - m)
            den.append(jnp.sum(e, axis=-1, keepdims=True) + jnp.exp(sink - m))
            p.append(e.astype(BF16))
        yield
        o = [_dot(p[kh], vband) / den[kh] for kh in range(SWA_KV_HEADS)]
        yc_ref[rows, g * SWA_KV_W:(g + 1) * SWA_KV_W] = jnp.where(low, o[0], o[1]).astype(BF16)

    attn_kv()
    for j in range(d // pc):
        attn_q(j)
    nq = d // pc
    n_blocks = tokens // SWA_BLOCK
    n_chunks = tokens // chunk
    attn = [(("attn", n, g), 0, 3, attention(n, g))
            for n in range(n_blocks) for g in range(SWA_GROUP)]
    ret = [(("ret", c, hd), 4 * (hd + 1), 2, retention(c, hd))
           for c in range(n_chunks) for hd in range(RET_HEADS)]
    free = frozenset()
    dense = []
    for hd in range(RET_HEADS):
        dense += [(functools.partial(rotary, hd), free),
                  (functools.partial(rotary, RET_HEADS + hd), free),
                  (functools.partial(ret_v, hd), free), (functools.partial(ret_gate, hd), free)]
    dense += [(functools.partial(conv, j), free) for j in range(nq)]
    _interleave(attn + ret, dense)
    kvp_ref[...] = skv_ref[tokens - SWA_BLOCK:, :].astype(F32)

    for j in range(nq):
        merge(j)

    o_ref[...] = h_ref[...] + g2_ref[...] * _dot(mg_ref[...], wo_ref[...])


def _mixer(h, shift, scale, gate, norm_g, w_in, w_q, conv_w, cos, sin, tables,
           p_ret, p_conv, p_swa, w_out, layer):
    bsz, seq, d = h.shape
    t = min(MIX_TOKENS, seq)
    chunk = min(RET_CHUNK, t)
    tok = pl.BlockSpec((None, t, d), lambda b, i: (b, i, 0))
    rope = pl.BlockSpec((t, ROPE_HALF), lambda b, i: (i, 0))
    intra, xi, zeta, decay, bias, sinks = tables
    small = [intra, xi, zeta, decay, bias]
    stacked = [p_ret, p_conv, p_swa, w_out]
    return pl.pallas_call(
        functools.partial(_mixer_kernel, tokens=t, chunk=chunk),
        grid=(bsz, seq // t),
        in_specs=[tok, _row_vec(d), _row_vec(d), _row_vec(d),
                  _resident(norm_g.shape), _resident_layer(w_in.shape, layer),
                  _resident_layer(w_q.shape, layer), _resident(conv_w.shape), rope, rope]
                 + [_resident(a.shape) for a in small]
                 + [pl.BlockSpec(memory_space=pltpu.SMEM)]
                 + [_resident_layer(a.shape, layer) for a in stacked],
        out_specs=tok,
        out_shape=jax.ShapeDtypeStruct((bsz, seq, d), F32),
        scratch_shapes=[
            pltpu.VMEM((t, d), BF16),
            pltpu.VMEM((8, d), F32),
            pltpu.VMEM((t, 2 * RET_WIDTH), BF16),
            pltpu.VMEM((t, RET_WIDTH), BF16),
            pltpu.VMEM((t, RET_WIDTH), BF16),
            pltpu.VMEM((t, d), BF16),
            pltpu.VMEM((t, d), BF16),
            pltpu.VMEM((t, 2 * SWA_KV_W), BF16),
            pltpu.VMEM((RET_HEADS, RET_DIM, RET_DIM), F32),
            pltpu.VMEM((SWA_BLOCK, 2 * SWA_KV_W), F32),
            pltpu.VMEM((t, d), BF16),
            pltpu.VMEM((t, d), BF16),
            pltpu.VMEM((t, d), BF16),
        ],
        compiler_params=pltpu.CompilerParams(
            dimension_semantics=("arbitrary", "arbitrary"),
            vmem_limit_bytes=VMEM_LIMIT_BYTES),
        name="hybrid_mixer",
    )(h, shift, scale, gate, norm_g, w_in, w_q, conv_w, cos, sin, *small, sinks, *stacked)


def _rope_tables(seq):
    inv = jnp.power(ROPE_BASE, -jnp.linspace(0.0, 1.0, ROPE_HALF, dtype=F32))
    ang = jnp.arange(seq, dtype=F32)[:, None] * inv[None, :]
    return jnp.cos(ang), jnp.sin(ang)


def _retention_tables(chunk):
    log_gamma = jnp.log1p(-jnp.exp2(-5.0 - jnp.arange(RET_HEADS, dtype=F32)))
    idx = jnp.arange(chunk, dtype=F32)
    rel = idx[:, None] - idx[None, :]
    intra = jnp.where(rel >= 0, jnp.exp(log_gamma[:, None, None] * jnp.maximum(rel, 0.0)), 0.0)
    xi = jnp.exp(log_gamma[:, None] * (idx + 1.0))
    zeta = jnp.exp(log_gamma[:, None] * (chunk - 1.0 - idx))
    decay = jnp.exp(log_gamma * chunk)
    wide = (RET_HEADS, chunk, RET_DIM)
    return (intra,
            jnp.broadcast_to(xi[:, :, None], wide),
            jnp.broadcast_to(zeta[:, :, None], wide),
            jnp.broadcast_to(decay[:, None, None], (RET_HEADS, 1, RET_DIM)))


def _swa_bias():
    w = SWA_BLOCK
    i = jnp.arange(w)[:, None]
    j = jnp.arange(2 * w)[None, :]
    diff = i + w - j
    in_band = (diff >= 0) & (diff < w)
    neg = jnp.float32(-jnp.inf)
    general = jnp.where(in_band, 0.0, neg)
    first = jnp.where(in_band & (j >= w), 0.0, neg)
    return jnp.stack([first, general]).astype(F32)


def _group_major(w, axis):
    shape = w.shape
    split = shape[:axis] + (SWA_KV_HEADS, SWA_GROUP, SWA_DIM) + shape[axis + 1:]
    perm = list(range(len(split)))
    perm[axis], perm[axis + 1] = perm[axis + 1], perm[axis]
    return w.reshape(split).transpose(perm).reshape(shape)


def kernel(x, c, ada_w, ada_b, ffn1_norm, ffn1_w1, ffn1_w3, ffn1_w2, mix_norm, w_in,
           conv_w, swa_sinks, p_ret, p_conv, p_swa, w_out, ffn2_norm, ffn2_w1, ffn2_w3,
           ffn2_w2, final_norm):
    bsz, seq, d = x.shape
    depth = ada_w.shape[0]
    mod = _modulation(c, ada_w, ada_b).reshape(depth, N_MOD, bsz, 1, d)
    cos, sin = _rope_tables(seq)
    ret_tables = _retention_tables(min(RET_CHUNK, MIX_TOKENS, seq))
    bias = _swa_bias()

    ffn1 = [w.astype(BF16) for w in (ffn1_w1, ffn1_w3, ffn1_w2)]
    ffn2 = [w.astype(BF16) for w in (ffn2_w1, ffn2_w3, ffn2_w2)]
    w_in_b = w_in.astype(BF16)
    w_q = _group_major(w_in[:, :, OFF_SQ:OFF_SKV], 2).astype(BF16)
    outs = [p_ret.astype(BF16), p_conv.astype(BF16), _group_major(p_swa, 1).astype(BF16),
            w_out.astype(BF16)]
    sinks = swa_sinks.astype(F32).reshape(depth, SWA_KV_HEADS, SWA_GROUP)

    h = x
    for l in range(depth):
        sh1, sc1, g1, sh2, sc2, g2, sh3, sc3, g3 = [mod[l, j] for j in range(N_MOD)]
        h = _ffn(h, sh1, sc1, g1, ffn1_norm[l][None, :], *ffn1, l)
        h = _mixer(h, sh2, sc2, g2, mix_norm[l][None, :], w_in_b, w_q, conv_w[l], cos, sin,
                   ret_tables + (bias, sinks[l]), *outs, l)
        fin = final_norm[None, :] if l == depth - 1 else None
        h = _ffn(h, sh3, sc3, g3, ffn2_norm[l][None, :], *ffn2, l, final_g=fin)
    return h
```

```python
import functools

import jax
import jax.numpy as jnp
from jax import lax
from jax.experimental import pallas as pl
from jax.experimental.pallas import tpu as pltpu

F32 = jnp.float32
BF16 = jnp.bfloat16

D_MODEL = 1024
N_MOD = 9
EPS = 1e-6
MACARON_WEIGHT = 0.5

RET_HEADS = 4
RET_DIM = 256
RET_WIDTH = RET_HEADS * RET_DIM
ROPE_BASE = 10000.0
ROPE_HALF = RET_DIM // 2

CONV_K = 3

SWA_DIM = 64
SWA_Q_HEADS = 16
SWA_KV_HEADS = 2
SWA_GROUP = SWA_Q_HEADS // SWA_KV_HEADS
SWA_BLOCK = 128
SWA_KV_W = SWA_KV_HEADS * SWA_DIM

OFF_RQ = 0
OFF_RK = RET_WIDTH
OFF_RV = 2 * RET_WIDTH
OFF_RG = 3 * RET_WIDTH
OFF_CB = 4 * RET_WIDTH
OFF_CC = OFF_CB + D_MODEL
OFF_CX = OFF_CC + D_MODEL
OFF_SQ = OFF_CX + D_MODEL
OFF_SKV = OFF_SQ + SWA_Q_HEADS * SWA_DIM
OFF_GATE = OFF_SKV + 2 * SWA_KV_W
IN_COLS = OFF_GATE + 3 * D_MODEL

VMEM_LIMIT_BYTES = 56 * 1024 * 1024

FFN_TOKENS = 1024
FFN_CHUNK = 256
NEXT_INPUT_PIECES = 8
MIX_TOKENS = 256
RET_CHUNK = 256


def _resident(shape):
    nd = len(shape)
    return pl.BlockSpec(shape, lambda *_: (0,) * nd, pipeline_mode=pl.Buffered(1))


def _resident_layer(stacked_shape, layer):
    nd = len(stacked_shape) - 1
    return pl.BlockSpec((None,) + tuple(stacked_shape[1:]), lambda *_: (layer,) + (0,) * nd,
                        pipeline_mode=pl.Buffered(1))


def _norm_modulate(x, g, shift, scale):
    ms = jnp.mean(x * x, axis=-1, keepdims=True)
    n = x * lax.rsqrt(ms + EPS)
    return (n * g) * (1.0 + scale) + shift


def _silu(x):
    return x * jax.nn.sigmoid(x)


def _dot(a, b):
    return jnp.dot(a, b, preferred_element_type=F32)


def _dot_nt(a, b):
    return lax.dot_general(a, b, (((1,), (1,)), ((), ())), preferred_element_type=F32)


def _dot_tn(a, b):
    return lax.dot_general(a, b, (((0,), (0,)), ((), ())), preferred_element_type=F32)


def _mod_kernel(c_ref, w_ref, b_ref, o_ref):
    ca = _silu(c_ref[...]).astype(BF16)
    o_ref[...] = _dot(ca, w_ref[...].astype(BF16)) + b_ref[...]


def _modulation(c, ada_w, ada_b):
    depth = ada_w.shape[0]
    bsz, d = c.shape
    b4 = ada_b.reshape(depth, N_MOD, 1, d)
    return pl.pallas_call(
        _mod_kernel,
        grid=(depth, N_MOD),
        in_specs=[
            pl.BlockSpec((bsz, d), lambda l, j: (0, 0)),
            pl.BlockSpec((None, d, d), lambda l, j: (l, 0, j)),
            pl.BlockSpec((None, None, 1, d), lambda l, j: (l, j, 0, 0)),
        ],
        out_specs=pl.BlockSpec((None, None, bsz, d), lambda l, j: (l, j, 0, 0)),
        out_shape=jax.ShapeDtypeStruct((depth, N_MOD, bsz, d), F32),
        compiler_params=pltpu.CompilerParams(
            dimension_semantics=("arbitrary", "arbitrary")),
        name="adaln_modulation",
    )(c, ada_w, b4)


def _ffn_kernel(h_ref, hn_ref, sh_ref, sc_ref, shn_ref, scn_ref, gt_ref, g_ref,
                w1_ref, w3_ref, w2_ref, *rest, n_chunks, chunk, final):
    if final:
        fg_ref, o_ref, u_ref, acc_ref = rest
    else:
        o_ref, u_ref, acc_ref = rest
    t = pl.program_id(0)
    slot = t % 2

    @pl.when(t == 0)
    def _():
        u_ref[0] = _norm_modulate(h_ref[...], g_ref[...], sh_ref[...], sc_ref[...]).astype(BF16)

    tm = h_ref.shape[0]
    pieces = min(NEXT_INPUT_PIECES, n_chunks)
    rows_per = tm // pieces
    first_piece_chunk = max(0, (n_chunks - pieces) // 2)

    for j in range(n_chunks):
        cols = slice(j * chunk, (j + 1) * chunk)
        u = u_ref[slot]
        a = _dot(u, w1_ref[:, cols])
        b = _dot(u, w3_ref[:, cols])
        act = (_silu(a) * b).astype(BF16)
        part = _dot(act, w2_ref[cols, :])
        if j == 0:
            acc_ref[...] = part
        else:
            acc_ref[...] += part
        i = j - first_piece_chunk
        if 0 <= i < pieces:
            rows = slice(i * rows_per, (i + 1) * rows_per)
            u_ref[1 - slot, rows, :] = _norm_modulate(
                hn_ref[rows, :], g_ref[...], shn_ref[...], scn_ref[...]).astype(BF16)

    y = h_ref[...] + (MACARON_WEIGHT * gt_ref[...]) * acc_ref[...]
    if final:
        ms = jnp.mean(y * y, axis=-1, keepdims=True)
        y = (y * lax.rsqrt(ms + EPS)) * fg_ref[...]
    o_ref[...] = y


def _tile_specs(bsz, seq, t, d):
    per_seq = seq // t
    n_tiles = bsz * per_seq

    def nxt(i):
        return jnp.minimum(i + 1, n_tiles - 1)

    tile = pl.BlockSpec((None, t, d), lambda i: (i // per_seq, i % per_seq, 0))
    tile_next = pl.BlockSpec((None, t, d), lambda i: (nxt(i) // per_seq, nxt(i) % per_seq, 0))
    vec = pl.BlockSpec((None, 1, d), lambda i: (i // per_seq, 0, 0))
    vec_next = pl.BlockSpec((None, 1, d), lambda i: (nxt(i) // per_seq, 0, 0))
    return n_tiles, per_seq, tile, tile_next, vec, vec_next


def _ffn(h, shift, scale, gate, norm_g, w1, w3, w2, layer, final_g=None):
    bsz, seq, d = h.shape
    dff = w1.shape[2]
    chunk = min(FFN_CHUNK, dff)
    tm = min(FFN_TOKENS, seq)
    n_tiles, _, tile, tile_next, vec, vec_next = _tile_specs(bsz, seq, tm, d)
    final = final_g is not None
    in_specs = [
        tile, tile_next, vec, vec, vec_next, vec_next, vec,
        _resident((1, d)),
        _resident_layer(w1.shape, layer),
        _resident_layer(w3.shape, layer),
        _resident_layer(w2.shape, layer),
    ]
    args = [h, h, shift, scale, shift, scale, gate, norm_g, w1, w3, w2]
    if final:
        in_specs.append(_resident((1, d)))
        args.append(final_g)
    return pl.pallas_call(
        functools.partial(_ffn_kernel, n_chunks=dff // chunk, chunk=chunk, final=final),
        grid=(n_tiles,),
        in_specs=in_specs,
        out_specs=tile,
        out_shape=jax.ShapeDtypeStruct((bsz, seq, d), F32),
        scratch_shapes=[pltpu.VMEM((2, tm, d), BF16), pltpu.VMEM((tm, d), F32)],
        compiler_params=pltpu.CompilerParams(
            dimension_semantics=("arbitrary",),
            vmem_limit_bytes=VMEM_LIMIT_BYTES),
        name="swiglu_ffn_final" if final else "swiglu_ffn",
    )(*args)


PROJ_CHUNK = 256
ACTIVE_CHAINS = 4


def _interleave(chains, dense):
    chains = list(chains)
    dense = list(dense)
    stages_left = sum(c[2] for c in chains)
    issued = 0
    active = []
    done = set()
    while chains or active or dense:
        ready = [c for c in chains if c[1] <= issued]
        for c in ready[:ACTIVE_CHAINS - len(active)]:
            chains.remove(c)
            active.append(c)
        for c in list(active):
            stages_left -= 1
            if next(c[3], "done") == "done":
                active.remove(c)
                done.add(c[0])
        ticks_left = max(1, -(-stages_left // ACTIVE_CHAINS))
        budget = -(-len(dense) // ticks_left)
        if not active and chains:
            budget = max(budget, min(c[1] for c in chains) - issued)
        for _ in range(budget):
            item = next((it for it in dense if it[1] <= done), None)
            if item is None:
                break
            dense.remove(item)
            item[0]()
            issued += 1


def _mixer_kernel(h_ref, hn_ref, sh_ref, sc_ref, shn_ref, scn_ref, g2_ref, g_ref, w_ref, wq_ref,
                  cw_ref, cos_ref, sin_ref,
                  intra_ref, xi_ref, zeta_ref, decay_ref, bias_ref, sink_ref,
                  pr_ref, pc_ref, ps_ref, wo_ref, o_ref,
                  u_ref, zc_ref, qk_ref, v_ref, sg_ref, yb_ref, sq_ref, skv_ref,
                  state_ref, kvp_ref, ya_ref, yc_ref, mg_ref, *, tokens, chunk, per_seq):
    d = D_MODEL
    pc = PROJ_CHUNK
    t = pl.program_id(0)
    slot = t % 2
    first = t % per_seq == 0

    @pl.when(first)
    def _():
        zc_ref[...] = jnp.zeros_like(zc_ref)
        state_ref[...] = jnp.zeros_like(state_ref)
        kvp_ref[...] = jnp.zeros_like(kvp_ref)

    @pl.when(t == 0)
    def _():
        u_ref[0] = _norm_modulate(h_ref[...], g_ref[...], sh_ref[...], sc_ref[...]).astype(BF16)

    def next_input(i, pieces):
        rows = slice(i * tokens // pieces, (i + 1) * tokens // pieces)
        u_ref[1 - slot, rows, :] = _norm_modulate(
            hn_ref[rows, :], g_ref[...], shn_ref[...], scn_ref[...]).astype(BF16)

    def proj(off, width=pc):
        return _dot(u_ref[slot], w_ref[:, off:off + width])

    def attn_kv():
        skv_ref[...] = proj(OFF_SKV, 2 * SWA_KV_W).astype(BF16)

    def attn_q(j):
        q = _dot(u_ref[slot], wq_ref[:, j * pc:(j + 1) * pc])
        sq_ref[:, j * pc:(j + 1) * pc] = (q * (SWA_DIM ** -0.5)).astype(BF16)

    def rotary(hd):
        p = proj(hd * RET_DIM, RET_DIM)
        cos = cos_ref[...]
        sin = sin_ref[...]
        p1 = p[:, :ROPE_HALF]
        p2 = p[:, ROPE_HALF:]
        o1 = p1 * cos - p2 * sin
        o2 = p2 * cos + p1 * sin
        if hd >= RET_HEADS:
            o1 = o1 * (RET_DIM ** -0.5)
            o2 = o2 * (RET_DIM ** -0.5)
        qk_ref[:, hd * RET_DIM:hd * RET_DIM + ROPE_HALF] = o1.astype(BF16)
        qk_ref[:, hd * RET_DIM + ROPE_HALF:(hd + 1) * RET_DIM] = o2.astype(BF16)

    def ret_v(j):
        v_ref[:, j * pc:(j + 1) * pc] = proj(OFF_RV + j * pc).astype(BF16)

    def ret_gate(j):
        sg_ref[:, j * pc:(j + 1) * pc] = _silu(proj(OFF_RG + j * pc)).astype(BF16)

    def conv(j):
        cols = slice(j * pc, (j + 1) * pc)
        z = proj(OFF_CC + j * pc) * proj(OFF_CX + j * pc)
        zcat = jnp.concatenate([zc_ref[:, cols], z], axis=0)
        z1 = pltpu.roll(zcat, 1, axis=0)[8:]
        z2 = pltpu.roll(zcat, 2, axis=0)[8:]
        y = z2 * cw_ref[0:1, cols] + z1 * cw_ref[1:2, cols] + z * cw_ref[2:3, cols]
        zc_ref[:, cols] = z[tokens - 8:]
        yb_ref[:, cols] = (proj(OFF_CB + j * pc) * y).astype(BF16)

    def merge(j):
        cols = slice(j * pc, (j + 1) * pc)

        def gate(branch):
            return jax.nn.sigmoid(proj(OFF_GATE + branch * d + j * pc))

        acc = gate(1) * _dot(yb_ref[...], pc_ref[:, cols])
        acc += gate(2) * _dot(yc_ref[...], ps_ref[:, cols])
        acc += gate(0) * _dot(ya_ref[...], pr_ref[:, cols])
        mg_ref[:, cols] = acc.astype(BF16)

    def retention(c, hd):
        cols = slice(hd * RET_DIM, (hd + 1) * RET_DIM)
        crows = slice(c * chunk, (c + 1) * chunk)
        k = qk_ref[crows, RET_WIDTH + hd * RET_DIM:RET_WIDTH + (hd + 1) * RET_DIM]
        v = v_ref[crows, cols]
        st = state_ref[hd]
        stb = st.astype(BF16)
        half = chunk // 2
        scores, cross = [], []
        for i in range(2):
            rows = slice(c * chunk + i * half, c * chunk + (i + 1) * half)
            sub = slice(i * half, (i + 1) * half)
            q = qk_ref[rows, hd * RET_DIM:(hd + 1) * RET_DIM]
            scores.append((_dot_nt(q, k) * intra_ref[hd, sub, :]).astype(BF16))
            cross.append(_dot(q, stb) * xi_ref[hd, sub, :])
        yield
        kz = (k.astype(F32) * zeta_ref[hd]).astype(BF16)
        state_ref[hd] = st * decay_ref[hd] + _dot_tn(kz, v)
        for i in range(2):
            rows = slice(c * chunk + i * half, c * chunk + (i + 1) * half)
            r = _dot(scores[i], v) + cross[i]
            rn = r * lax.rsqrt(jnp.mean(r * r, axis=-1, keepdims=True) + EPS)
            ya_ref[rows, cols] = (sg_ref[rows, cols].astype(F32) * rn).astype(BF16)

    lane = lax.broadcasted_iota(jnp.int32, (1, SWA_KV_W), 1)
    low = lane < SWA_DIM

    def attention(n, g):
        rows = slice(n * SWA_BLOCK, (n + 1) * SWA_BLOCK)
        q = sq_ref[rows, g * SWA_KV_W:(g + 1) * SWA_KV_W]
        kv_cur = skv_ref[rows, :]
        if n == 0:
            kv_prev = kvp_ref[...].astype(BF16)
            bias = bias_ref[jnp.where(first, 0, 1)]
        else:
            kv_prev = skv_ref[(n - 1) * SWA_BLOCK:n * SWA_BLOCK, :]
            bias = bias_ref[1]
        kband = jnp.concatenate([kv_prev[:, :SWA_KV_W], kv_cur[:, :SWA_KV_W]], axis=0)
        vband = jnp.concatenate([kv_prev[:, SWA_KV_W:], kv_cur[:, SWA_KV_W:]], axis=0)
        s = []
        for kh in range(SWA_KV_HEADS):
            keep = low if kh == 0 else jnp.logical_not(low)
            km = jnp.where(keep, kband, jnp.zeros_like(kband))
            s.append(_dot_nt(q, km) + bias)
        yield
        p, den = [], []
        for kh in range(SWA_KV_HEADS):
            sink = sink_ref[kh, g]
            m = jnp.maximum(jnp.max(s[kh], axis=-1, keepdims=True), sink)
            e = jnp.exp(s[kh] - m)
            den.append(jnp.sum(e, axis=-1, keepdims=True) + jnp.exp(sink - m))
            p.append(e.astype(BF16))
        yield
        o = [_dot(p[kh], vband) / den[kh] for kh in range(SWA_KV_HEADS)]
        yc_ref[rows, g * SWA_KV_W:(g + 1) * SWA_KV_W] = jnp.where(low, o[0], o[1]).astype(BF16)

    attn_kv()
    for j in range(d // pc):
        attn_q(j)
    nq = d // pc
    n_blocks = tokens // SWA_BLOCK
    n_chunks = tokens // chunk
    attn = [(("attn", n, g), 0, 3, attention(n, g))
            for n in range(n_blocks) for g in range(SWA_GROUP)]
    ret = [(("ret", c, hd), 4 * (hd + 1), 2, retention(c, hd))
           for c in range(n_chunks) for hd in range(RET_HEADS)]
    free = frozenset()
    dense = []
    for hd in range(RET_HEADS):
        dense += [(functools.partial(rotary, hd), free),
                  (functools.partial(rotary, RET_HEADS + hd), free),
                  (functools.partial(ret_v, hd), free), (functools.partial(ret_gate, hd), free)]
    dense += [(functools.partial(conv, j), free) for j in range(nq)]
    _interleave(attn + ret, dense)
    kvp_ref[...] = skv_ref[tokens - SWA_BLOCK:, :].astype(F32)

    for j in range(nq):
        merge(j)
        next_input(j, nq)

    o_ref[...] = h_ref[...] + g2_ref[...] * _dot(mg_ref[...], wo_ref[...])


def _mixer(h, shift, scale, gate, norm_g, w_in, w_q, conv_w, cos, sin, tables,
           p_ret, p_conv, p_swa, w_out, layer):
    bsz, seq, d = h.shape
    t = min(MIX_TOKENS, seq)
    chunk = min(RET_CHUNK, t)
    n_tiles, per_seq, tok, tok_next, vec, vec_next = _tile_specs(bsz, seq, t, d)
    rope = pl.BlockSpec((t, ROPE_HALF), lambda i: (i % per_seq, 0))
    intra, xi, zeta, decay, bias, sinks = tables
    small = [intra, xi, zeta, decay, bias]
    stacked = [p_ret, p_conv, p_swa, w_out]
    return pl.pallas_call(
        functools.partial(_mixer_kernel, tokens=t, chunk=chunk, per_seq=per_seq),
        grid=(n_tiles,),
        in_specs=[tok, tok_next, vec, vec, vec_next, vec_next, vec,
                  _resident(norm_g.shape), _resident_layer(w_in.shape, layer),
                  _resident_layer(w_q.shape, layer), _resident(conv_w.shape), rope, rope]
                 + [_resident(a.shape) for a in small]
                 + [pl.BlockSpec(memory_space=pltpu.SMEM)]
                 + [_resident_layer(a.shape, layer) for a in stacked],
        out_specs=tok,
        out_shape=jax.ShapeDtypeStruct((bsz, seq, d), F32),
        scratch_shapes=[
            pltpu.VMEM((2, t, d), BF16),
            pltpu.VMEM((8, d), F32),
            pltpu.VMEM((t, 2 * RET_WIDTH), BF16),
            pltpu.VMEM((t, RET_WIDTH), BF16),
            pltpu.VMEM((t, RET_WIDTH), BF16),
            pltpu.VMEM((t, d), BF16),
            pltpu.VMEM((t, d), BF16),
            pltpu.VMEM((t, 2 * SWA_KV_W), BF16),
            pltpu.VMEM((RET_HEADS, RET_DIM, RET_DIM), F32),
            pltpu.VMEM((SWA_BLOCK, 2 * SWA_KV_W), F32),
            pltpu.VMEM((t, d), BF16),
            pltpu.VMEM((t, d), BF16),
            pltpu.VMEM((t, d), BF16),
        ],
        compiler_params=pltpu.CompilerParams(
            dimension_semantics=("arbitrary",),
            vmem_limit_bytes=VMEM_LIMIT_BYTES),
        name="hybrid_mixer",
    )(h, h, shift, scale, shift, scale, gate, norm_g, w_in, w_q, conv_w, cos, sin, *small, sinks,
      *stacked)


def _rope_tables(seq):
    inv = jnp.power(ROPE_BASE, -jnp.linspace(0.0, 1.0, ROPE_HALF, dtype=F32))
    ang = jnp.arange(seq, dtype=F32)[:, None] * inv[None, :]
    return jnp.cos(ang), jnp.sin(ang)


def _retention_tables(chunk):
    log_gamma = jnp.log1p(-jnp.exp2(-5.0 - jnp.arange(RET_HEADS, dtype=F32)))
    idx = jnp.arange(chunk, dtype=F32)
    rel = idx[:, None] - idx[None, :]
    intra = jnp.where(rel >= 0, jnp.exp(log_gamma[:, None, None] * jnp.maximum(rel, 0.0)), 0.0)
    xi = jnp.exp(log_gamma[:, None] * (idx + 1.0))
    zeta = jnp.exp(log_gamma[:, None] * (chunk - 1.0 - idx))
    decay = jnp.exp(log_gamma * chunk)
    wide = (RET_HEADS, chunk, RET_DIM)
    return (intra,
            jnp.broadcast_to(xi[:, :, None], wide),
            jnp.broadcast_to(zeta[:, :, None], wide),
            jnp.broadcast_to(decay[:, None, None], (RET_HEADS, 1, RET_DIM)))


def _swa_bias():
    w = SWA_BLOCK
    i = jnp.arange(w)[:, None]
    j = jnp.arange(2 * w)[None, :]
    diff = i + w - j
    in_band = (diff >= 0) & (diff < w)
    neg = jnp.float32(-jnp.inf)
    general = jnp.where(in_band, 0.0, neg)
    first = jnp.where(in_band & (j >= w), 0.0, neg)
    return jnp.stack([first, general]).astype(F32)


def _group_major(w, axis):
    shape = w.shape
    split = shape[:axis] + (SWA_KV_HEADS, SWA_GROUP, SWA_DIM) + shape[axis + 1:]
    perm = list(range(len(split)))
    perm[axis], perm[axis + 1] = perm[axis + 1], perm[axis]
    return w.reshape(split).transpose(perm).reshape(shape)


def kernel(x, c, ada_w, ada_b, ffn1_norm, ffn1_w1, ffn1_w3, ffn1_w2, mix_norm, w_in,
           conv_w, swa_sinks, p_ret, p_conv, p_swa, w_out, ffn2_norm, ffn2_w1, ffn2_w3,
           ffn2_w2, final_norm):
    bsz, seq, d = x.shape
    depth = ada_w.shape[0]
    mod = _modulation(c, ada_w, ada_b).reshape(depth, N_MOD, bsz, 1, d)
    cos, sin = _rope_tables(seq)
    ret_tables = _retention_tables(min(RET_CHUNK, MIX_TOKENS, seq))
    bias = _swa_bias()

    ffn1 = [w.astype(BF16) for w in (ffn1_w1, ffn1_w3, ffn1_w2)]
    ffn2 = [w.astype(BF16) for w in (ffn2_w1, ffn2_w3, ffn2_w2)]
    w_in_b = w_in.astype(BF16)
    w_q = _group_major(w_in[:, :, OFF_SQ:OFF_SKV], 2).astype(BF16)
    outs = [p_ret.astype(BF16), p_conv.astype(BF16), _group_major(p_swa, 1).astype(BF16),
            w_out.astype(BF16)]
    sinks = swa_sinks.astype(F32).reshape(depth, SWA_KV_HEADS, SWA_GROUP)

    h = x
    for l in range(depth):
        sh1, sc1, g1, sh2, sc2, g2, sh3, sc3, g3 = [mod[l, j] for j in range(N_MOD)]
        h = _ffn(h, sh1, sc1, g1, ffn1_norm[l][None, :], *ffn1, l)
        h = _mixer(h, sh2, sc2, g2, mix_norm[l][None, :], w_in_b, w_q, conv_w[l], cos, sin,
                   ret_tables + (bias, sinks[l]), *outs, l)
        fin = final_norm[None, :] if l == depth - 1 else None
        h = _ffn(h, sh3, sc3, g3, ffn2_norm[l][None, :], *ffn2, l, final_g=fin)
    return h
```

```python
import functools

import jax
import jax.numpy as jnp
from jax import lax
from jax.experimental import pallas as pl
from jax.experimental.pallas import tpu as pltpu

F32 = jnp.float32
BF16 = jnp.bfloat16

D_MODEL = 1024
N_MOD = 9
EPS = 1e-6
MACARON_WEIGHT = 0.5

RET_HEADS = 4
RET_DIM = 256
RET_WIDTH = RET_HEADS * RET_DIM
ROPE_BASE = 10000.0
ROPE_HALF = RET_DIM // 2

CONV_K = 3

SWA_DIM = 64
SWA_Q_HEADS = 16
SWA_KV_HEADS = 2
SWA_GROUP = SWA_Q_HEADS // SWA_KV_HEADS
SWA_BLOCK = 128
SWA_KV_W = SWA_KV_HEADS * SWA_DIM

OFF_RQ = 0
OFF_RK = RET_WIDTH
OFF_RV = 2 * RET_WIDTH
OFF_RG = 3 * RET_WIDTH
OFF_CB = 4 * RET_WIDTH
OFF_CC = OFF_CB + D_MODEL
OFF_CX = OFF_CC + D_MODEL
OFF_SQ = OFF_CX + D_MODEL
OFF_SKV = OFF_SQ + SWA_Q_HEADS * SWA_DIM
OFF_GATE = OFF_SKV + 2 * SWA_KV_W
IN_COLS = OFF_GATE + 3 * D_MODEL

VMEM_LIMIT_BYTES = 56 * 1024 * 1024

FFN_TOKENS = 1024
FFN_CHUNK = 256
NEXT_INPUT_PIECES = 8
MIX_TOKENS = 256
RET_CHUNK = 256


def _resident(shape):
    nd = len(shape)
    return pl.BlockSpec(shape, lambda *_: (0,) * nd, pipeline_mode=pl.Buffered(1))


def _resident_layer(stacked_shape, layer):
    nd = len(stacked_shape) - 1
    return pl.BlockSpec((None,) + tuple(stacked_shape[1:]), lambda *_: (layer,) + (0,) * nd,
                        pipeline_mode=pl.Buffered(1))


def _row_vec(d):
    return pl.BlockSpec((None, 1, d), lambda b, i: (b, 0, 0))


def _norm_modulate(x, g, shift, scale):
    ms = jnp.mean(x * x, axis=-1, keepdims=True)
    n = x * lax.rsqrt(ms + EPS)
    return (n * g) * (1.0 + scale) + shift


def _silu(x):
    return x * jax.nn.sigmoid(x)


def _dot(a, b):
    return jnp.dot(a, b, preferred_element_type=F32)


def _dot_nt(a, b):
    return lax.dot_general(a, b, (((1,), (1,)), ((), ())), preferred_element_type=F32)


def _dot_tn(a, b):
    return lax.dot_general(a, b, (((0,), (0,)), ((), ())), preferred_element_type=F32)


def _mod_kernel(c_ref, w_ref, b_ref, o_ref):
    ca = _silu(c_ref[...]).astype(BF16)
    o_ref[...] = _dot(ca, w_ref[...].astype(BF16)) + b_ref[...]


def _modulation(c, ada_w, ada_b):
    depth = ada_w.shape[0]
    bsz, d = c.shape
    b4 = ada_b.reshape(depth, N_MOD, 1, d)
    return pl.pallas_call(
        _mod_kernel,
        grid=(depth, N_MOD),
        in_specs=[
            pl.BlockSpec((bsz, d), lambda l, j: (0, 0)),
            pl.BlockSpec((None, d, d), lambda l, j: (l, 0, j)),
            pl.BlockSpec((None, None, 1, d), lambda l, j: (l, j, 0, 0)),
        ],
        out_specs=pl.BlockSpec((None, None, bsz, d), lambda l, j: (l, j, 0, 0)),
        out_shape=jax.ShapeDtypeStruct((depth, N_MOD, bsz, d), F32),
        compiler_params=pltpu.CompilerParams(
            dimension_semantics=("arbitrary", "arbitrary")),
        name="adaln_modulation",
    )(c, ada_w, b4)


def _ffn_kernel(h_ref, hn_ref, sh_ref, sc_ref, shn_ref, scn_ref, gt_ref, g_ref,
                w1_ref, w3_ref, w2_ref, *rest, n_chunks, chunk, final):
    if final:
        fg_ref, o_ref, u_ref, acc_ref = rest
    else:
        o_ref, u_ref, acc_ref = rest
    t = pl.program_id(0)
    slot = t % 2

    @pl.when(t == 0)
    def _():
        u_ref[0] = _norm_modulate(h_ref[...], g_ref[...], sh_ref[...], sc_ref[...]).astype(BF16)

    tm = h_ref.shape[0]
    pieces = min(NEXT_INPUT_PIECES, n_chunks)
    rows_per = tm // pieces
    first_piece_chunk = max(0, (n_chunks - pieces) // 2)

    for j in range(n_chunks):
        cols = slice(j * chunk, (j + 1) * chunk)
        u = u_ref[slot]
        a = _dot(u, w1_ref[:, cols])
        b = _dot(u, w3_ref[:, cols])
        act = (_silu(a) * b).astype(BF16)
        part = _dot(act, w2_ref[cols, :])
        if j == 0:
            acc_ref[...] = part
        else:
            acc_ref[...] += part
        i = j - first_piece_chunk
        if 0 <= i < pieces:
            rows = slice(i * rows_per, (i + 1) * rows_per)
            u_ref[1 - slot, rows, :] = _norm_modulate(
                hn_ref[rows, :], g_ref[...], shn_ref[...], scn_ref[...]).astype(BF16)

    y = h_ref[...] + (MACARON_WEIGHT * gt_ref[...]) * acc_ref[...]
    if final:
        ms = jnp.mean(y * y, axis=-1, keepdims=True)
        y = (y * lax.rsqrt(ms + EPS)) * fg_ref[...]
    o_ref[...] = y


def _tile_specs(bsz, seq, t, d):
    per_seq = seq // t
    n_tiles = bsz * per_seq

    def nxt(i):
        return jnp.minimum(i + 1, n_tiles - 1)

    tile = pl.BlockSpec((None, t, d), lambda i: (i // per_seq, i % per_seq, 0))
    tile_next = pl.BlockSpec((None, t, d), lambda i: (nxt(i) // per_seq, nxt(i) % per_seq, 0))
    vec = pl.BlockSpec((None, 1, d), lambda i: (i // per_seq, 0, 0))
    vec_next = pl.BlockSpec((None, 1, d), lambda i: (nxt(i) // per_seq, 0, 0))
    return n_tiles, tile, tile_next, vec, vec_next


def _ffn(h, shift, scale, gate, norm_g, w1, w3, w2, layer, final_g=None):
    bsz, seq, d = h.shape
    dff = w1.shape[2]
    chunk = min(FFN_CHUNK, dff)
    tm = min(FFN_TOKENS, seq)
    n_tiles, tile, tile_next, vec, vec_next = _tile_specs(bsz, seq, tm, d)
    final = final_g is not None
    in_specs = [
        tile, tile_next, vec, vec, vec_next, vec_next, vec,
        _resident((1, d)),
        _resident_layer(w1.shape, layer),
        _resident_layer(w3.shape, layer),
        _resident_layer(w2.shape, layer),
    ]
    args = [h, h, shift, scale, shift, scale, gate, norm_g, w1, w3, w2]
    if final:
        in_specs.append(_resident((1, d)))
        args.append(final_g)
    return pl.pallas_call(
        functools.partial(_ffn_kernel, n_chunks=dff // chunk, chunk=chunk, final=final),
        grid=(n_tiles,),
        in_specs=in_specs,
        out_specs=tile,
        out_shape=jax.ShapeDtypeStruct((bsz, seq, d), F32),
        scratch_shapes=[pltpu.VMEM((2, tm, d), BF16), pltpu.VMEM((tm, d), F32)],
        compiler_params=pltpu.CompilerParams(
            dimension_semantics=("arbitrary",),
            vmem_limit_bytes=VMEM_LIMIT_BYTES),
        name="swiglu_ffn_final" if final else "swiglu_ffn",
    )(*args)


PROJ_CHUNK = 256
ACTIVE_CHAINS = 4


def _interleave(chains, dense):
    chains = list(chains)
    dense = list(dense)
    stages_left = sum(c[1] for c in chains)
    issued = 0
    active = []
    while chains or active or dense:
        ready = [c for c in chains if c[0] <= issued]
        for c in ready[:ACTIVE_CHAINS - len(active)]:
            chains.remove(c)
            active.append(c[2])
        for gen in list(active):
            stages_left -= 1
            if next(gen, "done") == "done":
                active.remove(gen)
        ticks_left = max(1, -(-stages_left // ACTIVE_CHAINS))
        budget = -(-len(dense) // ticks_left)
        if not active and chains:
            budget = max(budget, min(c[0] for c in chains) - issued)
        for _ in range(min(budget, len(dense))):
            dense.pop(0)()
            issued += 1


def _mixer_kernel(h_ref, sh_ref, sc_ref, g2_ref, g_ref, w_ref, wq_ref, cw_ref, cos_ref, sin_ref,
                  intra_ref, xi_ref, zeta_ref, decay_ref, bias_ref, sink_ref,
                  pr_ref, pc_ref, ps_ref, wo_ref, o_ref,
                  u_ref, zc_ref, qk_ref, v_ref, sg_ref, yb_ref, sq_ref, skv_ref,
                  state_ref, kvp_ref, ya_ref, yc_ref, mg_ref, *, tokens, chunk):
    d = D_MODEL
    pc = PROJ_CHUNK
    first = pl.program_id(1) == 0

    @pl.when(first)
    def _():
        zc_ref[...] = jnp.zeros_like(zc_ref)
        state_ref[...] = jnp.zeros_like(state_ref)
        kvp_ref[...] = jnp.zeros_like(kvp_ref)

    u_ref[...] = _norm_modulate(h_ref[...], g_ref[...], sh_ref[...], sc_ref[...]).astype(BF16)

    def proj(off, width=pc):
        return _dot(u_ref[...], w_ref[:, off:off + width])

    def attn_kv():
        skv_ref[...] = proj(OFF_SKV, 2 * SWA_KV_W).astype(BF16)

    def attn_q(j):
        q = _dot(u_ref[...], wq_ref[:, j * pc:(j + 1) * pc])
        sq_ref[:, j * pc:(j + 1) * pc] = (q * (SWA_DIM ** -0.5)).astype(BF16)

    def rotary(hd):
        p = proj(hd * RET_DIM, RET_DIM)
        cos = cos_ref[...]
        sin = sin_ref[...]
        p1 = p[:, :ROPE_HALF]
        p2 = p[:, ROPE_HALF:]
        o1 = p1 * cos - p2 * sin
        o2 = p2 * cos + p1 * sin
        if hd >= RET_HEADS:
            o1 = o1 * (RET_DIM ** -0.5)
            o2 = o2 * (RET_DIM ** -0.5)
        qk_ref[:, hd * RET_DIM:hd * RET_DIM + ROPE_HALF] = o1.astype(BF16)
        qk_ref[:, hd * RET_DIM + ROPE_HALF:(hd + 1) * RET_DIM] = o2.astype(BF16)

    def ret_v(j):
        v_ref[:, j * pc:(j + 1) * pc] = proj(OFF_RV + j * pc).astype(BF16)

    def ret_gate(j):
        sg_ref[:, j * pc:(j + 1) * pc] = _silu(proj(OFF_RG + j * pc)).astype(BF16)

    def conv(j):
        cols = slice(j * pc, (j + 1) * pc)
        z = proj(OFF_CC + j * pc) * proj(OFF_CX + j * pc)
        zcat = jnp.concatenate([zc_ref[:, cols], z], axis=0)
        z1 = pltpu.roll(zcat, 1, axis=0)[8:]
        z2 = pltpu.roll(zcat, 2, axis=0)[8:]
        y = z2 * cw_ref[0:1, cols] + z1 * cw_ref[1:2, cols] + z * cw_ref[2:3, cols]
        zc_ref[:, cols] = z[tokens - 8:]
        yb_ref[:, cols] = (proj(OFF_CB + j * pc) * y).astype(BF16)

    def merge(j):
        cols = slice(j * pc, (j + 1) * pc)

        def gate(branch):
            return jax.nn.sigmoid(proj(OFF_GATE + branch * d + j * pc))

        acc = gate(1) * _dot(yb_ref[...], pc_ref[:, cols])
        acc += gate(2) * _dot(yc_ref[...], ps_ref[:, cols])
        acc += gate(0) * _dot(ya_ref[...], pr_ref[:, cols])
        mg_ref[:, cols] = acc.astype(BF16)

    def retention(c, hd):
        cols = slice(hd * RET_DIM, (hd + 1) * RET_DIM)
        crows = slice(c * chunk, (c + 1) * chunk)
        k = qk_ref[crows, RET_WIDTH + hd * RET_DIM:RET_WIDTH + (hd + 1) * RET_DIM]
        v = v_ref[crows, cols]
        st = state_ref[hd]
        stb = st.astype(BF16)
        half = chunk // 2
        scores, cross = [], []
        for i in range(2):
            rows = slice(c * chunk + i * half, c * chunk + (i + 1) * half)
            sub = slice(i * half, (i + 1) * half)
            q = qk_ref[rows, hd * RET_DIM:(hd + 1) * RET_DIM]
            scores.append((_dot_nt(q, k) * intra_ref[hd, sub, :]).astype(BF16))
            cross.append(_dot(q, stb) * xi_ref[hd, sub, :])
        yield
        kz = (k.astype(F32) * zeta_ref[hd]).astype(BF16)
        state_ref[hd] = st * decay_ref[hd] + _dot_tn(kz, v)
        for i in range(2):
            rows = slice(c * chunk + i * half, c * chunk + (i + 1) * half)
            r = _dot(scores[i], v) + cross[i]
            rn = r * lax.rsqrt(jnp.mean(r * r, axis=-1, keepdims=True) + EPS)
            ya_ref[rows, cols] = (sg_ref[rows, cols].astype(F32) * rn).astype(BF16)

    lane = lax.broadcasted_iota(jnp.int32, (1, SWA_KV_W), 1)
    low = lane < SWA_DIM

    def attention(n, g):
        rows = slice(n * SWA_BLOCK, (n + 1) * SWA_BLOCK)
        q = sq_ref[rows, g * SWA_KV_W:(g + 1) * SWA_KV_W]
        kv_cur = skv_ref[rows, :]
        if n == 0:
            kv_prev = kvp_ref[...].astype(BF16)
            bias = bias_ref[jnp.where(first, 0, 1)]
        else:
            kv_prev = skv_ref[(n - 1) * SWA_BLOCK:n * SWA_BLOCK, :]
            bias = bias_ref[1]
        kband = jnp.concatenate([kv_prev[:, :SWA_KV_W], kv_cur[:, :SWA_KV_W]], axis=0)
        vband = jnp.concatenate([kv_prev[:, SWA_KV_W:], kv_cur[:, SWA_KV_W:]], axis=0)
        s = []
        for kh in range(SWA_KV_HEADS):
            keep = low if kh == 0 else jnp.logical_not(low)
            km = jnp.where(keep, kband, jnp.zeros_like(kband))
            s.append(_dot_nt(q, km) + bias)
        yield
        p, den = [], []
        for kh in range(SWA_KV_HEADS):
            sink = sink_ref[kh, g]
            m = jnp.maximum(jnp.max(s[kh], axis=-1, keepdims=True), sink)
            e = jnp.exp(s[kh] - m)
            den.append(jnp.sum(e, axis=-1, keepdims=True) + jnp.exp(sink - m))
            p.append(e.astype(BF16))
        yield
        o = [_dot(p[kh], vband) / den[kh] for kh in range(SWA_KV_HEADS)]
        yc_ref[rows, g * SWA_KV_W:(g + 1) * SWA_KV_W] = jnp.where(low, o[0], o[1]).astype(BF16)

    attn_kv()
    for j in range(d // pc):
        attn_q(j)
    nq = d // pc
    n_blocks = tokens // SWA_BLOCK
    n_chunks = tokens // chunk
    attn = [(0, 3, attention(n, g)) for n in range(n_blocks) for g in range(SWA_GROUP)]
    ret = [(4 * (hd + 1), 2, retention(c, hd))
           for c in range(n_chunks) for hd in range(RET_HEADS)]
    dense = []
    for hd in range(RET_HEADS):
        dense += [functools.partial(rotary, hd), functools.partial(rotary, RET_HEADS + hd),
                  functools.partial(ret_v, hd), functools.partial(ret_gate, hd)]
    dense += [functools.partial(conv, j) for j in range(nq)]
    _interleave(attn + ret, dense)
    kvp_ref[...] = skv_ref[tokens - SWA_BLOCK:, :].astype(F32)

    for j in range(nq):
        merge(j)

    o_ref[...] = h_ref[...] + g2_ref[...] * _dot(mg_ref[...], wo_ref[...])


def _mixer(h, shift, scale, gate, norm_g, w_in, w_q, conv_w, cos, sin, tables,
           p_ret, p_conv, p_swa, w_out, layer):
    bsz, seq, d = h.shape
    t = min(MIX_TOKENS, seq)
    chunk = min(RET_CHUNK, t)
    tok = pl.BlockSpec((None, t, d), lambda b, i: (b, i, 0))
    rope = pl.BlockSpec((t, ROPE_HALF), lambda b, i: (i, 0))
    intra, xi, zeta, decay, bias, sinks = tables
    small = [intra, xi, zeta, decay, bias]
    stacked = [p_ret, p_conv, p_swa, w_out]
    return pl.pallas_call(
        functools.partial(_mixer_kernel, tokens=t, chunk=chunk),
        grid=(bsz, seq // t),
        in_specs=[tok, _row_vec(d), _row_vec(d), _row_vec(d),
                  _resident(norm_g.shape), _resident_layer(w_in.shape, layer),
                  _resident_layer(w_q.shape, layer), _resident(conv_w.shape), rope, rope]
                 + [_resident(a.shape) for a in small]
                 + [pl.BlockSpec(memory_space=pltpu.SMEM)]
                 + [_resident_layer(a.shape, layer) for a in stacked],
        out_specs=tok,
        out_shape=jax.ShapeDtypeStruct((bsz, seq, d), F32),
        scratch_shapes=[
            pltpu.VMEM((t, d), BF16),
            pltpu.VMEM((8, d), F32),
            pltpu.VMEM((t, 2 * RET_WIDTH), BF16),
            pltpu.VMEM((t, RET_WIDTH), BF16),
            pltpu.VMEM((t, RET_WIDTH), BF16),
            pltpu.VMEM((t, d), BF16),
            pltpu.VMEM((t, d), BF16),
            pltpu.VMEM((t, 2 * SWA_KV_W), BF16),
            pltpu.VMEM((RET_HEADS, RET_DIM, RET_DIM), F32),
            pltpu.VMEM((SWA_BLOCK, 2 * SWA_KV_W), F32),
            pltpu.VMEM((t, d), BF16),
            pltpu.VMEM((t, d), BF16),
            pltpu.VMEM((t, d), BF16),
        ],
        compiler_params=pltpu.CompilerParams(
            dimension_semantics=("arbitrary", "arbitrary"),
            vmem_limit_bytes=VMEM_LIMIT_BYTES),
        name="hybrid_mixer",
    )(h, shift, scale, gate, norm_g, w_in, w_q, conv_w, cos, sin, *small, sinks, *stacked)


def _rope_tables(seq):
    inv = jnp.power(ROPE_BASE, -jnp.linspace(0.0, 1.0, ROPE_HALF, dtype=F32))
    ang = jnp.arange(seq, dtype=F32)[:, None] * inv[None, :]
    return jnp.cos(ang), jnp.sin(ang)


def _retention_tables(chunk):
    log_gamma = jnp.log1p(-jnp.exp2(-5.0 - jnp.arange(RET_HEADS, dtype=F32)))
    idx = jnp.arange(chunk, dtype=F32)
    rel = idx[:, None] - idx[None, :]
    intra = jnp.where(rel >= 0, jnp.exp(log_gamma[:, None, None] * jnp.maximum(rel, 0.0)), 0.0)
    xi = jnp.exp(log_gamma[:, None] * (idx + 1.0))
    zeta = jnp.exp(log_gamma[:, None] * (chunk - 1.0 - idx))
    decay = jnp.exp(log_gamma * chunk)
    wide = (RET_HEADS, chunk, RET_DIM)
    return (intra,
            jnp.broadcast_to(xi[:, :, None], wide),
            jnp.broadcast_to(zeta[:, :, None], wide),
            jnp.broadcast_to(decay[:, None, None], (RET_HEADS, 1, RET_DIM)))


def _swa_bias():
    w = SWA_BLOCK
    i = jnp.arange(w)[:, None]
    j = jnp.arange(2 * w)[None, :]
    diff = i + w - j
    in_band = (diff >= 0) & (diff < w)
    neg = jnp.float32(-jnp.inf)
    general = jnp.where(in_band, 0.0, neg)
    first = jnp.where(in_band & (j >= w), 0.0, neg)
    return jnp.stack([first, general]).astype(F32)


def _group_major(w, axis):
    shape = w.shape
    split = shape[:axis] + (SWA_KV_HEADS, SWA_GROUP, SWA_DIM) + shape[axis + 1:]
    perm = list(range(len(split)))
    perm[axis], perm[axis + 1] = perm[axis + 1], perm[axis]
    return w.reshape(split).transpose(perm).reshape(shape)


def kernel(x, c, ada_w, ada_b, ffn1_norm, ffn1_w1, ffn1_w3, ffn1_w2, mix_norm, w_in,
           conv_w, swa_sinks, p_ret, p_conv, p_swa, w_out, ffn2_norm, ffn2_w1, ffn2_w3,
           ffn2_w2, final_norm):
    bsz, seq, d = x.shape
    depth = ada_w.shape[0]
    mod = _modulation(c, ada_w, ada_b).reshape(depth, N_MOD, bsz, 1, d)
    cos, sin = _rope_tables(seq)
    ret_tables = _retention_tables(min(RET_CHUNK, MIX_TOKENS, seq))
    bias = _swa_bias()

    ffn1 = [w.astype(BF16) for w in (ffn1_w1, ffn1_w3, ffn1_w2)]
    ffn2 = [w.astype(BF16) for w in (ffn2_w1, ffn2_w3, ffn2_w2)]
    w_in_b = w_in.astype(BF16)
    w_q = _group_major(w_in[:, :, OFF_SQ:OFF_SKV], 2).astype(BF16)
    outs = [p_ret.astype(BF16), p_conv.astype(BF16), _group_major(p_swa, 1).astype(BF16),
            w_out.astype(BF16)]
    sinks = swa_sinks.astype(F32).reshape(depth, SWA_KV_HEADS, SWA_GROUP)

    h = x
    for l in range(depth):
        sh1, sc1, g1, sh2, sc2, g2, sh3, sc3, g3 = [mod[l, j] for j in range(N_MOD)]
        h = _ffn(h, sh1, sc1, g1, ffn1_norm[l][None, :], *ffn1, l)
        h = _mixer(h, sh2, sc2, g2, mix_norm[l][None, :], w_in_b, w_q, conv_w[l], cos, sin,
                   ret_tables + (bias, sinks[l]), *outs, l)
        fin = final_norm[None, :] if l == depth - 1 else None
        h = _ffn(h, sh3, sc3, g3, ffn2_norm[l][None, :], *ffn2, l, final_g=fin)
    return h
```

```python
import functools

import jax
import jax.numpy as jnp
from jax import lax
from jax.experimental import pallas as pl
from jax.experimental.pallas import tpu as pltpu

F32 = jnp.float32
BF16 = jnp.bfloat16

D_MODEL = 1024
N_MOD = 9
EPS = 1e-6
MACARON_WEIGHT = 0.5

RET_HEADS = 4
RET_DIM = 256
RET_WIDTH = RET_HEADS * RET_DIM
ROPE_BASE = 10000.0
ROPE_HALF = RET_DIM // 2

CONV_K = 3

SWA_DIM = 64
SWA_Q_HEADS = 16
SWA_KV_HEADS = 2
SWA_GROUP = SWA_Q_HEADS // SWA_KV_HEADS
SWA_BLOCK = 128
SWA_KV_W = SWA_KV_HEADS * SWA_DIM

OFF_RQ = 0
OFF_RK = RET_WIDTH
OFF_RV = 2 * RET_WIDTH
OFF_RG = 3 * RET_WIDTH
OFF_CB = 4 * RET_WIDTH
OFF_CC = OFF_CB + D_MODEL
OFF_CX = OFF_CC + D_MODEL
OFF_SQ = OFF_CX + D_MODEL
OFF_SKV = OFF_SQ + SWA_Q_HEADS * SWA_DIM
OFF_GATE = OFF_SKV + 2 * SWA_KV_W
IN_COLS = OFF_GATE + 3 * D_MODEL

VMEM_LIMIT_BYTES = 56 * 1024 * 1024

FFN_TOKENS = 1024
FFN_CHUNK = 256
NEXT_INPUT_PIECES = 8
MIX_TOKENS = 256
RET_CHUNK = 256


def _resident(shape):
    nd = len(shape)
    return pl.BlockSpec(shape, lambda *_: (0,) * nd, pipeline_mode=pl.Buffered(1))


def _resident_layer(stacked_shape, layer):
    nd = len(stacked_shape) - 1
    return pl.BlockSpec((None,) + tuple(stacked_shape[1:]), lambda *_: (layer,) + (0,) * nd,
                        pipeline_mode=pl.Buffered(1))


def _norm_modulate(x, g, shift, scale):
    ms = jnp.mean(x * x, axis=-1, keepdims=True)
    n = x * lax.rsqrt(ms + EPS)
    return (n * g) * (1.0 + scale) + shift


def _silu(x):
    return x * jax.nn.sigmoid(x)


def _dot(a, b):
    return jnp.dot(a, b, preferred_element_type=F32)


def _dot_nt(a, b):
    return lax.dot_general(a, b, (((1,), (1,)), ((), ())), preferred_element_type=F32)


def _dot_tn(a, b):
    return lax.dot_general(a, b, (((0,), (0,)), ((), ())), preferred_element_type=F32)


def _mod_kernel(c_ref, w_ref, b_ref, o_ref):
    ca = _silu(c_ref[...]).astype(BF16)
    o_ref[...] = _dot(ca, w_ref[...].astype(BF16)) + b_ref[...]


def _modulation(c, ada_w, ada_b):
    depth = ada_w.shape[0]
    bsz, d = c.shape
    b4 = ada_b.reshape(depth, N_MOD, 1, d)
    return pl.pallas_call(
        _mod_kernel,
        grid=(depth, N_MOD),
        in_specs=[
            pl.BlockSpec((bsz, d), lambda l, j: (0, 0)),
            pl.BlockSpec((None, d, d), lambda l, j: (l, 0, j)),
            pl.BlockSpec((None, None, 1, d), lambda l, j: (l, j, 0, 0)),
        ],
        out_specs=pl.BlockSpec((None, None, bsz, d), lambda l, j: (l, j, 0, 0)),
        out_shape=jax.ShapeDtypeStruct((depth, N_MOD, bsz, d), F32),
        compiler_params=pltpu.CompilerParams(
            dimension_semantics=("arbitrary", "arbitrary")),
        name="adaln_modulation",
    )(c, ada_w, b4)


def _ffn_kernel(h_ref, hn_ref, sh_ref, sc_ref, shn_ref, scn_ref, gt_ref, g_ref,
                w1_ref, w3_ref, w2_ref, *rest, n_chunks, chunk, final):
    if final:
        fg_ref, o_ref, u_ref, acc_ref = rest
    else:
        o_ref, u_ref, acc_ref = rest
    t = pl.program_id(0)
    slot = t % 2

    @pl.when(t == 0)
    def _():
        u_ref[0] = _norm_modulate(h_ref[...], g_ref[...], sh_ref[...], sc_ref[...]).astype(BF16)

    tm = h_ref.shape[0]
    pieces = min(NEXT_INPUT_PIECES, n_chunks)
    rows_per = tm // pieces
    first_piece_chunk = max(0, (n_chunks - pieces) // 2)

    for j in range(n_chunks):
        cols = slice(j * chunk, (j + 1) * chunk)
        u = u_ref[slot]
        a = _dot(u, w1_ref[:, cols])
        b = _dot(u, w3_ref[:, cols])
        act = (_silu(a) * b).astype(BF16)
        part = _dot(act, w2_ref[cols, :])
        if j == 0:
            acc_ref[...] = part
        else:
            acc_ref[...] += part
        i = j - first_piece_chunk
        if 0 <= i < pieces:
            rows = slice(i * rows_per, (i + 1) * rows_per)
            u_ref[1 - slot, rows, :] = _norm_modulate(
                hn_ref[rows, :], g_ref[...], shn_ref[...], scn_ref[...]).astype(BF16)

    y = h_ref[...] + (MACARON_WEIGHT * gt_ref[...]) * acc_ref[...]
    if final:
        ms = jnp.mean(y * y, axis=-1, keepdims=True)
        y = (y * lax.rsqrt(ms + EPS)) * fg_ref[...]
    o_ref[...] = y


def _tile_specs(bsz, seq, t, d):
    per_seq = seq // t
    n_tiles = bsz * per_seq

    def nxt(i):
        return jnp.minimum(i + 1, n_tiles - 1)

    tile = pl.BlockSpec((None, t, d), lambda i: (i // per_seq, i % per_seq, 0))
    tile_next = pl.BlockSpec((None, t, d), lambda i: (nxt(i) // per_seq, nxt(i) % per_seq, 0))
    vec = pl.BlockSpec((None, 1, d), lambda i: (i // per_seq, 0, 0))
    vec_next = pl.BlockSpec((None, 1, d), lambda i: (nxt(i) // per_seq, 0, 0))
    return n_tiles, tile, tile_next, vec, vec_next


def _ffn(h, shift, scale, gate, norm_g, w1, w3, w2, layer, final_g=None):
    bsz, seq, d = h.shape
    dff = w1.shape[2]
    chunk = min(FFN_CHUNK, dff)
    tm = min(FFN_TOKENS, seq)
    n_tiles, tile, tile_next, vec, vec_next = _tile_specs(bsz, seq, tm, d)
    final = final_g is not None
    in_specs = [
        tile, tile_next, vec, vec, vec_next, vec_next, vec,
        _resident((1, d)),
        _resident_layer(w1.shape, layer),
        _resident_layer(w3.shape, layer),
        _resident_layer(w2.shape, layer),
    ]
    args = [h, h, shift, scale, shift, scale, gate, norm_g, w1, w3, w2]
    if final:
        in_specs.append(_resident((1, d)))
        args.append(final_g)
    return pl.pallas_call(
        functools.partial(_ffn_kernel, n_chunks=dff // chunk, chunk=chunk, final=final),
        grid=(n_tiles,),
        in_specs=in_specs,
        out_specs=tile,
        out_shape=jax.ShapeDtypeStruct((bsz, seq, d), F32),
        scratch_shapes=[pltpu.VMEM((2, tm, d), BF16), pltpu.VMEM((tm, d), F32)],
        compiler_params=pltpu.CompilerParams(
            dimension_semantics=("arbitrary",),
            vmem_limit_bytes=VMEM_LIMIT_BYTES),
        name="swiglu_ffn_final" if final else "swiglu_ffn",
    )(*args)


PROJ_CHUNK = 256
ACTIVE_CHAINS = 4


def _interleave(chains, dense):
    chains = list(chains)
    dense = list(dense)
    stages_left = sum(c[1] for c in chains)
    issued = 0
    active = []
    while chains or active or dense:
        ready = [c for c in chains if c[0] <= issued]
        for c in ready[:ACTIVE_CHAINS - len(active)]:
            chains.remove(c)
            active.append(c[2])
        for gen in list(active):
            stages_left -= 1
            if next(gen, "done") == "done":
                active.remove(gen)
        ticks_left = max(1, -(-stages_left // ACTIVE_CHAINS))
        budget = -(-len(dense) // ticks_left)
        if not active and chains:
            budget = max(budget, min(c[0] for c in chains) - issued)
        for _ in range(min(budget, len(dense))):
            dense.pop(0)()
            issued += 1


def _mixer_kernel(h_ref, mv_ref, g_ref, w_ref, wq_ref, cw_ref, rope_ref,
                  intra_ref, xi_ref, zeta_ref, decay_ref, bias_ref, sink_ref,
                  pr_ref, pc_ref, ps_ref, wo_ref, o_ref,
                  u_ref, zc_ref, qk_ref, v_ref, sg_ref, yb_ref, sq_ref, skv_ref,
                  state_ref, kvp_ref, ya_ref, yc_ref, mg_ref, *, tokens, chunk):
    d = D_MODEL
    pc = PROJ_CHUNK
    first = pl.program_id(1) == 0

    @pl.when(first)
    def _():
        zc_ref[...] = jnp.zeros_like(zc_ref)
        state_ref[...] = jnp.zeros_like(state_ref)
        kvp_ref[...] = jnp.zeros_like(kvp_ref)

    u_ref[...] = _norm_modulate(h_ref[...], g_ref[...], mv_ref[0:1, :], mv_ref[1:2, :]).astype(BF16)

    def proj(off, width=pc):
        return _dot(u_ref[...], w_ref[:, off:off + width])

    def attn_kv():
        skv_ref[...] = proj(OFF_SKV, 2 * SWA_KV_W).astype(BF16)

    def attn_q(j):
        q = _dot(u_ref[...], wq_ref[:, j * pc:(j + 1) * pc])
        sq_ref[:, j * pc:(j + 1) * pc] = (q * (SWA_DIM ** -0.5)).astype(BF16)

    def rotary(hd):
        p = proj(hd * RET_DIM, RET_DIM)
        cos = rope_ref[:, :ROPE_HALF]
        sin = rope_ref[:, ROPE_HALF:]
        p1 = p[:, :ROPE_HALF]
        p2 = p[:, ROPE_HALF:]
        o1 = p1 * cos - p2 * sin
        o2 = p2 * cos + p1 * sin
        if hd >= RET_HEADS:
            o1 = o1 * (RET_DIM ** -0.5)
            o2 = o2 * (RET_DIM ** -0.5)
        qk_ref[:, hd * RET_DIM:hd * RET_DIM + ROPE_HALF] = o1.astype(BF16)
        qk_ref[:, hd * RET_DIM + ROPE_HALF:(hd + 1) * RET_DIM] = o2.astype(BF16)

    def ret_v(j):
        v_ref[:, j * pc:(j + 1) * pc] = proj(OFF_RV + j * pc).astype(BF16)

    def ret_gate(j):
        sg_ref[:, j * pc:(j + 1) * pc] = _silu(proj(OFF_RG + j * pc)).astype(BF16)

    def conv(j):
        cols = slice(j * pc, (j + 1) * pc)
        z = proj(OFF_CC + j * pc) * proj(OFF_CX + j * pc)
        zcat = jnp.concatenate([zc_ref[:, cols], z], axis=0)
        z1 = pltpu.roll(zcat, 1, axis=0)[8:]
        z2 = pltpu.roll(zcat, 2, axis=0)[8:]
        y = z2 * cw_ref[0:1, cols] + z1 * cw_ref[1:2, cols] + z * cw_ref[2:3, cols]
        zc_ref[:, cols] = z[tokens - 8:]
        yb_ref[:, cols] = (proj(OFF_CB + j * pc) * y).astype(BF16)

    def merge(j):
        cols = slice(j * pc, (j + 1) * pc)

        def gate(branch):
            return jax.nn.sigmoid(proj(OFF_GATE + branch * d + j * pc))

        acc = gate(1) * _dot(yb_ref[...], pc_ref[:, cols])
        acc += gate(2) * _dot(yc_ref[...], ps_ref[:, cols])
        acc += gate(0) * _dot(ya_ref[...], pr_ref[:, cols])
        mg_ref[:, cols] = acc.astype(BF16)

    def retention(c, hd):
        cols = slice(hd * RET_DIM, (hd + 1) * RET_DIM)
        crows = slice(c * chunk, (c + 1) * chunk)
        k = qk_ref[crows, RET_WIDTH + hd * RET_DIM:RET_WIDTH + (hd + 1) * RET_DIM]
        v = v_ref[crows, cols]
        st = state_ref[hd]
        stb = st.astype(BF16)
        half = chunk // 2
        scores, cross = [], []
        for i in range(2):
            rows = slice(c * chunk + i * half, c * chunk + (i + 1) * half)
            sub = slice(i * half, (i + 1) * half)
            q = qk_ref[rows, hd * RET_DIM:(hd + 1) * RET_DIM]
            scores.append((_dot_nt(q, k) * intra_ref[hd, sub, :]).astype(BF16))
            cross.append(_dot(q, stb) * xi_ref[hd, sub, :])
        yield
        kz = (k.astype(F32) * zeta_ref[hd]).astype(BF16)
        state_ref[hd] = st * decay_ref[hd] + _dot_tn(kz, v)
        for i in range(2):
            rows = slice(c * chunk + i * half, c * chunk + (i + 1) * half)
            r = _dot(scores[i], v) + cross[i]
            rn = r * lax.rsqrt(jnp.mean(r * r, axis=-1, keepdims=True) + EPS)
            ya_ref[rows, cols] = (sg_ref[rows, cols].astype(F32) * rn).astype(BF16)

    lane = lax.broadcasted_iota(jnp.int32, (1, SWA_KV_W), 1)
    low = lane < SWA_DIM

    def attention(n, g):
        rows = slice(n * SWA_BLOCK, (n + 1) * SWA_BLOCK)
        q = sq_ref[rows, g * SWA_KV_W:(g + 1) * SWA_KV_W]
        kv_cur = skv_ref[rows, :]
        if n == 0:
            kv_prev = kvp_ref[...].astype(BF16)
            bias = bias_ref[jnp.where(first, 0, 1)]
        else:
            kv_prev = skv_ref[(n - 1) * SWA_BLOCK:n * SWA_BLOCK, :]
            bias = bias_ref[1]
        kband = jnp.concatenate([kv_prev[:, :SWA_KV_W], kv_cur[:, :SWA_KV_W]], axis=0)
        vband = jnp.concatenate([kv_prev[:, SWA_KV_W:], kv_cur[:, SWA_KV_W:]], axis=0)
        s = []
        for kh in range(SWA_KV_HEADS):
            keep = low if kh == 0 else jnp.logical_not(low)
            km = jnp.where(keep, kband, jnp.zeros_like(kband))
            s.append(_dot_nt(q, km) + bias)
        yield
        p, den = [], []
        for kh in range(SWA_KV_HEADS):
            sink = sink_ref[kh, g]
            m = jnp.maximum(jnp.max(s[kh], axis=-1, keepdims=True), sink)
            e = jnp.exp(s[kh] - m)
            den.append(jnp.sum(e, axis=-1, keepdims=True) + jnp.exp(sink - m))
            p.append(e.astype(BF16))
        yield
        o = [_dot(p[kh], vband) / den[kh] for kh in range(SWA_KV_HEADS)]
        yc_ref[rows, g * SWA_KV_W:(g + 1) * SWA_KV_W] = jnp.where(low, o[0], o[1]).astype(BF16)

    attn_kv()
    for j in range(d // pc):
        attn_q(j)
    nq = d // pc
    n_blocks = tokens // SWA_BLOCK
    n_chunks = tokens // chunk
    attn = [(0, 3, attention(n, g)) for n in range(n_blocks) for g in range(SWA_GROUP)]
    ret = [(4 * (hd + 1), 2, retention(c, hd))
           for c in range(n_chunks) for hd in range(RET_HEADS)]
    dense = []
    for hd in range(RET_HEADS):
        dense += [functools.partial(rotary, hd), functools.partial(rotary, RET_HEADS + hd),
                  functools.partial(ret_v, hd), functools.partial(ret_gate, hd)]
    dense += [functools.partial(conv, j) for j in range(nq)]
    _interleave(attn + ret, dense)
    kvp_ref[...] = skv_ref[tokens - SWA_BLOCK:, :].astype(F32)

    for j in range(nq):
        merge(j)

    o_ref[...] = h_ref[...] + mv_ref[2:3, :] * _dot(mg_ref[...], wo_ref[...])


def _mixer(h, mod_vecs, norm_g, w_in, w_q, conv_w, rope_tab, tables,
           p_ret, p_conv, p_swa, w_out, layer):
    bsz, seq, d = h.shape
    t = min(MIX_TOKENS, seq)
    chunk = min(RET_CHUNK, t)
    tok = pl.BlockSpec((None, t, d), lambda b, i: (b, i, 0))
    vecs = pl.BlockSpec((None, 3, d), lambda b, i: (b, 0, 0))
    rope = pl.BlockSpec((t, 2 * ROPE_HALF), lambda b, i: (i, 0))
    intra, xi, zeta, decay, bias, sinks = tables
    small = [intra, xi, zeta, decay, bias]
    stacked = [p_ret, p_conv, p_swa, w_out]
    return pl.pallas_call(
        functools.partial(_mixer_kernel, tokens=t, chunk=chunk),
        grid=(bsz, seq // t),
        in_specs=[tok, vecs,
                  _resident(norm_g.shape), _resident_layer(w_in.shape, layer),
                  _resident_layer(w_q.shape, layer), _resident(conv_w.shape), rope]
                 + [_resident(a.shape) for a in small]
                 + [pl.BlockSpec(memory_space=pltpu.SMEM)]
                 + [_resident_layer(a.shape, layer) for a in stacked],
        out_specs=tok,
        out_shape=jax.ShapeDtypeStruct((bsz, seq, d), F32),
        scratch_shapes=[
            pltpu.VMEM((t, d), BF16),
            pltpu.VMEM((8, d), F32),
            pltpu.VMEM((t, 2 * RET_WIDTH), BF16),
            pltpu.VMEM((t, RET_WIDTH), BF16),
            pltpu.VMEM((t, RET_WIDTH), BF16),
            pltpu.VMEM((t, d), BF16),
            pltpu.VMEM((t, d), BF16),
            pltpu.VMEM((t, 2 * SWA_KV_W), BF16),
            pltpu.VMEM((RET_HEADS, RET_DIM, RET_DIM), F32),
            pltpu.VMEM((SWA_BLOCK, 2 * SWA_KV_W), F32),
            pltpu.VMEM((t, d), BF16),
            pltpu.VMEM((t, d), BF16),
            pltpu.VMEM((t, d), BF16),
        ],
        compiler_params=pltpu.CompilerParams(
            dimension_semantics=("arbitrary", "arbitrary"),
            vmem_limit_bytes=VMEM_LIMIT_BYTES),
        name="hybrid_mixer",
    )(h, mod_vecs, norm_g, w_in, w_q, conv_w, rope_tab, *small, sinks, *stacked)


def _rope_tables(seq):
    inv = jnp.power(ROPE_BASE, -jnp.linspace(0.0, 1.0, ROPE_HALF, dtype=F32))
    ang = jnp.arange(seq, dtype=F32)[:, None] * inv[None, :]
    return jnp.cos(ang), jnp.sin(ang)


def _retention_tables(chunk):
    log_gamma = jnp.log1p(-jnp.exp2(-5.0 - jnp.arange(RET_HEADS, dtype=F32)))
    idx = jnp.arange(chunk, dtype=F32)
    rel = idx[:, None] - idx[None, :]
    intra = jnp.where(rel >= 0, jnp.exp(log_gamma[:, None, None] * jnp.maximum(rel, 0.0)), 0.0)
    xi = jnp.exp(log_gamma[:, None] * (idx + 1.0))
    zeta = jnp.exp(log_gamma[:, None] * (chunk - 1.0 - idx))
    decay = jnp.exp(log_gamma * chunk)
    wide = (RET_HEADS, chunk, RET_DIM)
    return (intra,
            jnp.broadcast_to(xi[:, :, None], wide),
            jnp.broadcast_to(zeta[:, :, None], wide),
            jnp.broadcast_to(decay[:, None, None], (RET_HEADS, 1, RET_DIM)))


def _swa_bias():
    w = SWA_BLOCK
    i = jnp.arange(w)[:, None]
    j = jnp.arange(2 * w)[None, :]
    diff = i + w - j
    in_band = (diff >= 0) & (diff < w)
    neg = jnp.float32(-jnp.inf)
    general = jnp.where(in_band, 0.0, neg)
    first = jnp.where(in_band & (j >= w), 0.0, neg)
    return jnp.stack([first, general]).astype(F32)


def _group_major(w, axis):
    shape = w.shape
    split = shape[:axis] + (SWA_KV_HEADS, SWA_GROUP, SWA_DIM) + shape[axis + 1:]
    perm = list(range(len(split)))
    perm[axis], perm[axis + 1] = perm[axis + 1], perm[axis]
    return w.reshape(split).transpose(perm).reshape(shape)


def kernel(x, c, ada_w, ada_b, ffn1_norm, ffn1_w1, ffn1_w3, ffn1_w2, mix_norm, w_in,
           conv_w, swa_sinks, p_ret, p_conv, p_swa, w_out, ffn2_norm, ffn2_w1, ffn2_w3,
           ffn2_w2, final_norm):
    bsz, seq, d = x.shape
    depth = ada_w.shape[0]
    mod = _modulation(c, ada_w, ada_b).reshape(depth, N_MOD, bsz, 1, d)
    rope_tab = jnp.concatenate(_rope_tables(seq), axis=1)
    ret_tables = _retention_tables(min(RET_CHUNK, MIX_TOKENS, seq))
    bias = _swa_bias()

    ffn1 = [w.astype(BF16) for w in (ffn1_w1, ffn1_w3, ffn1_w2)]
    ffn2 = [w.astype(BF16) for w in (ffn2_w1, ffn2_w3, ffn2_w2)]
    w_in_b = w_in.astype(BF16)
    w_q = _group_major(w_in[:, :, OFF_SQ:OFF_SKV], 2).astype(BF16)
    outs = [p_ret.astype(BF16), p_conv.astype(BF16), _group_major(p_swa, 1).astype(BF16),
            w_out.astype(BF16)]
    sinks = swa_sinks.astype(F32).reshape(depth, SWA_KV_HEADS, SWA_GROUP)

    h = x
    for l in range(depth):
        sh1, sc1, g1, sh2, sc2, g2, sh3, sc3, g3 = [mod[l, j] for j in range(N_MOD)]
        h = _ffn(h, sh1, sc1, g1, ffn1_norm[l][None, :], *ffn1, l)
        h = _mixer(h, jnp.concatenate([sh2, sc2, g2], axis=1), mix_norm[l][None, :], w_in_b, w_q,
                   conv_w[l], rope_tab, ret_tables + (bias, sinks[l]), *outs, l)
        fin = final_norm[None, :] if l == depth - 1 else None
        h = _ffn(h, sh3, sc3, g3, ffn2_norm[l][None, :], *ffn2, l, final_g=fin)
    return h
```
